```python
import math
import jax, jax.numpy as jnp
from jax import lax
import numpy as np

D_MODEL = 2048
BATCH = 4
SEQ = 2048
DEPTH = 4
DEC_BATCH = 128
DEC_SEQ = 8
PAST_LEN = 16384
PAGE_SIZE = 128

D_MIX = D_MODEL
D_M = D_MIX // 2
H_M = 8
DV_M = D_M // H_M
DK_M = DV_M // 2
D_R = D_MIX // 4
N_R = 64
H_R = D_R // N_R
W_LORA = max(32, int(round(1.8 * D_R ** 0.5 / 32)) * 32)
A_LORA = max(32, int(round(1.8 * D_R ** 0.5 / 32)) * 32)
G_LORA = max(32, int(round(0.6 * D_R ** 0.8 / 32)) * 32)
GN_EPS = 64e-5
D_G = D_MIX - D_M - D_R
H_G = 4
DV_G = D_G // H_G
K_G = D_G // 2
DK_G = K_G // H_G
ALPHA_LORA = 16
GLA_TAU = 16.0
CHUNK = 64
D_FF = 256 * ((8 * D_MODEL // 3 + 255) // 256)
NORM_EPS = 1e-6
P_M = 2 * H_M * DK_M + 2 * D_M + 2 * H_M
P_R = 3 * D_R + W_LORA + A_LORA + G_LORA
P_G = 2 * K_G + 2 * D_G + ALPHA_LORA
P_IN = P_M + P_R + P_G

kernel_name = "hymba_mlstm_rwkv7_gla_macaron_step"


def rmsnorm(x, g):
    xf = x.astype(jnp.float32)
    y = xf * lax.rsqrt(jnp.mean(xf * xf, axis=-1, keepdims=True) + NORM_EPS)
    return (y * g.astype(jnp.float32)).astype(x.dtype)


def swiglu(h, w_gate, w_up, w_down):
    return (jax.nn.silu(h @ w_gate) * (h @ w_up)) @ w_down


def split_cols(x, sizes):
    idx = [int(i) for i in np.cumsum(sizes)[:-1]]
    return jnp.split(x, idx, axis=-1)


def to_chunks(x, L):
    B, T = x.shape[0], x.shape[1]
    x = x.reshape((B, T // L, L) + x.shape[2:])
    return jnp.moveaxis(x, (1, 2), (0, 3))


def from_chunks(y):
    y = jnp.moveaxis(y, (0, 3), (1, 2))
    B, NC, L = y.shape[0], y.shape[1], y.shape[2]
    return y.reshape((B, NC * L) + y.shape[3:])


def mlstm_chunk(carry, xs):
    C, n, m = carry
    q, k, v, li, lf = xs
    L = q.shape[2]
    causal = jnp.tril(jnp.ones((L, L), dtype=bool))
    b = jnp.cumsum(lf, axis=-1)
    dmat = jnp.where(causal, b[..., :, None] - b[..., None, :] + li[..., None, :], -jnp.inf)
    inter = b + m[..., None]
    m_t = jnp.maximum(inter, jnp.max(dmat, axis=-1))
    w_intra = jnp.exp(dmat - m_t[..., None]) * jnp.einsum('bhtd,bhsd->bhts', q, k)
    w_inter = jnp.exp(inter - m_t)
    num = w_inter[..., None] * jnp.einsum('bhtd,bhde->bhte', q, C) + jnp.einsum('bhts,bhse->bhte', w_intra, v)
    den = w_inter * jnp.einsum('bhtd,bhd->bht', q, n) + jnp.sum(w_intra, axis=-1)
    h = num / jnp.maximum(jnp.abs(den), jnp.exp(-m_t))[..., None]
    g_end = b[..., -1]
    dec = g_end[..., None] - b + li
    m_new = jnp.maximum(g_end + m, jnp.max(dec, axis=-1))
    ws = jnp.exp(dec - m_new[..., None])
    sc = jnp.exp(g_end + m - m_new)
    C_new = sc[..., None, None] * C + jnp.einsum('bhs,bhsd,bhse->bhde', ws, k, v)
    n_new = sc[..., None] * n + jnp.einsum('bhs,bhsd->bhd', ws, k)
    return (C_new, n_new, m_new), h


def mlstm_group(pm, C0, n0, m0, b_i, b_f, norm_g):
    B, T, _ = pm.shape
    q, k, v, o, ig, fg = split_cols(pm, [H_M * DK_M, H_M * DK_M, D_M, D_M, H_M, H_M])
    q = q.reshape(B, T, H_M, DK_M)
    k = k.reshape(B, T, H_M, DK_M) * (DK_M ** -0.5)
    v = v.reshape(B, T, H_M, DV_M)
    li = ig + b_i
    lf = jax.nn.log_sigmoid(fg + b_f)
    L = math.gcd(T, CHUNK)
    (C, n, m), h = lax.scan(mlstm_chunk, (C0, n0, m0),
                            (to_chunks(q, L), to_chunks(k, L), to_chunks(v, L),
                             to_chunks(li, L), to_chunks(lf, L)))
    h = rmsnorm(from_chunks(h), norm_g.reshape(H_M, DV_M)).reshape(B, T, D_M)
    return h * jax.nn.sigmoid(o), C, n, m


def rwkv_group(pr, S0, shift0, mu, w0, w_up, a0, a_up, g_up, k_k, k_a, r_k, ln_w, ln_b):
    B, T, _ = pr.shape
    prev = jnp.concatenate([shift0[:, None, :], pr[:, :-1]], axis=1)
    xs = pr + (prev - pr) * mu
    r, xw, k, v, xa, xg = split_cols(xs, [D_R, W_LORA, D_R, D_R, A_LORA, G_LORA])
    w = -jax.nn.softplus(-(w0 + jnp.tanh(xw) @ w_up)) - 0.5
    decay = jnp.exp(-jnp.exp(w))
    a = jax.nn.sigmoid(a0 + xa @ a_up)
    g = jax.nn.sigmoid(xg) @ g_up
    heads = lambda z: z.reshape(B, T, H_R, N_R)
    kk = heads(k * k_k)
    kk = kk / jnp.maximum(jnp.sqrt(jnp.sum(kk * kk, axis=-1, keepdims=True)), 1e-12)
    k = k * (1.0 + (a - 1.0) * k_a)
    r, k, v, a, decay = heads(r), heads(k), heads(v), heads(a), heads(decay)

    def step(S, inp):
        r_t, w_t, k_t, v_t, kk_t, a_t = inp
        sa = jnp.einsum('bhij,bhj->bhi', S, -kk_t)
        S = (S * w_t[:, :, None, :] + sa[..., None] * (kk_t * a_t)[:, :, None, :]
             + v_t[..., None] * k_t[:, :, None, :])
        return S, jnp.einsum('bhij,bhj->bhi', S, r_t)

    tm = lambda z: jnp.swapaxes(z, 0, 1)
    S, y = lax.scan(step, S0, (tm(r), tm(decay), tm(k), tm(v), tm(kk), tm(a)))
    y = tm(y)
    mean = jnp.mean(y, axis=-1, keepdims=True)
    var = jnp.mean(jnp.square(y - mean), axis=-1, keepdims=True)
    y = (y - mean) * lax.rsqrt(var + GN_EPS) * ln_w.reshape(H_R, N_R) + ln_b.reshape(H_R, N_R)
    y = y + jnp.sum(r * k * r_k, axis=-1, keepdims=True) * v
    return y.reshape(B, T, D_R) * g, S, pr[:, -1]


def gla_chunk(S, xs):
    q, k, v, la = xs
    L = q.shape[2]
    causal = jnp.tril(jnp.ones((L, L), dtype=bool))
    bc = jnp.cumsum(la, axis=2)
    diff = jnp.where(causal[:, :, None], bc[:, :, :, None, :] - bc[:, :, None, :, :], -jnp.inf)
    att = jnp.einsum('bhtd,bhsd,bhtsd->bhts', q, k, jnp.exp(diff))
    o = jnp.einsum('bhts,bhse->bhte', att, v) + jnp.einsum('bhtd,bhde->bhte', q * jnp.exp(bc), S)
    b_end = bc[:, :, -1:, :]
    S_new = (jnp.exp(b_end[:, :, 0, :])[..., None] * S
             + jnp.einsum('bhsd,bhse->bhde', k * jnp.exp(b_end - bc), v))
    return S_new, o


def gla_group(pg, S0, alpha_up, alpha_b, norm_g):
    B, T, _ = pg.shape
    q, k, v, xa, gg = split_cols(pg, [K_G, K_G, D_G, ALPHA_LORA, D_G])
    la = jax.nn.log_sigmoid(xa @ alpha_up + alpha_b) / GLA_TAU
    q = q.reshape(B, T, H_G, DK_G) * (DK_G ** -0.5)
    k = k.reshape(B, T, H_G, DK_G)
    v = v.reshape(B, T, H_G, DV_G)
    la = la.reshape(B, T, H_G, DK_G)
    L = math.gcd(T, CHUNK)
    S, o = lax.scan(gla_chunk, S0, (to_chunks(q, L), to_chunks(k, L), to_chunks(v, L), to_chunks(la, L)))
    o = rmsnorm(from_chunks(o), norm_g.reshape(H_G, DV_G)).reshape(B, T, D_G)
    return o * jax.nn.silu(gg), S


def layer(x, state, p):
    C, n, m, Sr, sh, Sg = state
    h = rmsnorm(x, p['ffn1_pre_g'])
    x = x + 0.5 * rmsnorm(swiglu(h, p['ffn1_w_gate'], p['ffn1_w_up'], p['ffn1_w_down']), p['ffn1_post_g'])
    h = rmsnorm(x, p['mix_pre_g'])
    proj = (h @ p['w_in']).astype(jnp.float32)
    pm, pr, pg = split_cols(proj, [P_M, P_R, P_G])
    ym, C, n, m = mlstm_group(pm, C, n, m, p['mlstm_b_i'], p['mlstm_b_f'], p['mlstm_norm_g'])
    yr, Sr, sh = rwkv_group(pr, Sr, sh, p['rwkv_mu'], p['rwkv_w0'], p['rwkv_w_up'], p['rwkv_a0'],
                            p['rwkv_a_up'], p['rwkv_g_up'], p['rwkv_k_k'], p['rwkv_k_a'], p['rwkv_r_k'],
                            p['rwkv_ln_w'], p['rwkv_ln_b'])
    yg, Sg = gla_group(pg, Sg, p['gla_alpha_up'], p['gla_alpha_b'], p['gla_norm_g'])
    y = jnp.concatenate([ym, yr, yg], axis=-1).astype(x.dtype) @ p['w_out']
    x = x + rmsnorm(y, p['mix_post_g'])
    h = rmsnorm(x, p['ffn2_pre_g'])
    x = x + 0.5 * rmsnorm(swiglu(h, p['ffn2_w_gate'], p['ffn2_w_up'], p['ffn2_w_down']), p['ffn2_post_g'])
    return x, (C, n, m, Sr, sh, Sg)


def setup_inputs(seed: int = 0) -> dict:
    key = jax.random.key(seed)
    ks = iter(jax.random.split(key, 48))
    nrm = lambda shape, s: jax.random.normal(next(ks), shape, jnp.float32) * s
    uni = lambda shape, lo, hi: jax.random.uniform(next(ks), shape, jnp.float32, lo, hi)
    gain = lambda shape: 1.0 + nrm(shape, 0.02)
    L = DEPTH
    return {
        "x_prompt": nrm((BATCH, SEQ, D_MODEL), 1.0),
        "x_sample": nrm((DEC_BATCH, DEC_SEQ, D_MODEL), 1.0),
        "state_mlstm_c": nrm((L, DEC_BATCH, H_M, DK_M, DV_M), 0.1),
        "state_mlstm_n": nrm((L, DEC_BATCH, H_M, DK_M), 0.1),
        "state_mlstm_m": nrm((L, DEC_BATCH, H_M), 1.0),
        "state_rwkv_s": nrm((L, DEC_BATCH, H_R, N_R, N_R), 0.1),
        "state_rwkv_shift": nrm((L, DEC_BATCH, P_R), 1.0),
        "state_gla_s": nrm((L, DEC_BATCH, H_G, DK_G, DV_G), 0.1),
        "ffn1_pre_g": gain((L, D_MODEL)),
        "ffn1_post_g": gain((L, D_MODEL)),
        "ffn1_w_gate": nrm((L, D_MODEL, D_FF), D_MODEL ** -0.5),
        "ffn1_w_up": nrm((L, D_MODEL, D_FF), D_MODEL ** -0.5),
        "ffn1_w_down": nrm((L, D_FF, D_MODEL), D_FF ** -0.5),
        "mix_pre_g": gain((L, D_MODEL)),
        "mix_post_g": gain((L, D_MODEL)),
        "w_in": nrm((L, D_MODEL, P_IN), D_MODEL ** -0.5),
        "w_out": nrm((L, D_MIX, D_MODEL), D_MIX ** -0.5),
        "mlstm_b_i": nrm((L, H_M), 0.1),
        "mlstm_b_f": uni((L, H_M), 3.0, 6.0),
        "mlstm_norm_g": gain((L, D_M)),
        "rwkv_mu": uni((L, P_R), 0.0, 1.0),
        "rwkv_w0": uni((L, D_R), -2.0, 1.0),
        "rwkv_w_up": nrm((L, W_LORA, D_R), 0.1),
        "rwkv_a0": nrm((L, D_R), 0.1),
        "rwkv_a_up": nrm((L, A_LORA, D_R), 0.1),
        "rwkv_g_up": nrm((L, G_LORA, D_R), G_LORA ** -0.5),
        "rwkv_k_k": 0.85 + nrm((L, D_R), 0.02),
        "rwkv_k_a": gain((L, D_R)),
        "rwkv_r_k": nrm((L, H_R, N_R), 0.1),
        "rwkv_ln_w": gain((L, D_R)),
        "rwkv_ln_b": nrm((L, D_R), 0.02),
        "gla_alpha_up": nrm((L, ALPHA_LORA, K_G), ALPHA_LORA ** -0.5),
        "gla_alpha_b": nrm((L, K_G), 0.1),
        "gla_norm_g": gain((L, D_G)),
        "ffn2_pre_g": gain((L, D_MODEL)),
        "ffn2_post_g": gain((L, D_MODEL)),
        "ffn2_w_gate": nrm((L, D_MODEL, D_FF), D_MODEL ** -0.5),
        "ffn2_w_up": nrm((L, D_MODEL, D_FF), D_MODEL ** -0.5),
        "ffn2_w_down": nrm((L, D_FF, D_MODEL), D_FF ** -0.5),
    }


def reference(x_prompt, x_sample, state_mlstm_c, state_mlstm_n, state_mlstm_m, state_rwkv_s,
              state_rwkv_shift, state_gla_s, ffn1_pre_g, ffn1_post_g, ffn1_w_gate, ffn1_w_up,
              ffn1_w_down, mix_pre_g, mix_post_g, w_in, w_out, mlstm_b_i, mlstm_b_f, mlstm_norm_g,
              rwkv_mu, rwkv_w0, rwkv_w_up, rwkv_a0, rwkv_a_up, rwkv_g_up, rwkv_k_k, rwkv_k_a,
              rwkv_r_k, rwkv_ln_w, rwkv_ln_b, gla_alpha_up, gla_alpha_b, gla_norm_g,
              ffn2_pre_g, ffn2_post_g, ffn2_w_gate, ffn2_w_up, ffn2_w_down):
    f32 = jnp.float32
    bp = x_prompt.shape[0]
    zero_state = (jnp.zeros((bp, H_M, DK_M, DV_M), f32), jnp.zeros((bp, H_M, DK_M), f32),
                  jnp.zeros((bp, H_M), f32), jnp.zeros((bp, H_R, N_R, N_R), f32),
                  jnp.zeros((bp, P_R), f32), jnp.zeros((bp, H_G, DK_G, DV_G), f32))
    yp, ys = x_prompt, x_sample
    new_p = [[] for _ in range(6)]
    new_s = [[] for _ in range(6)]
    for l in range(DEPTH):
        p = dict(
            ffn1_pre_g=ffn1_pre_g[l], ffn1_post_g=ffn1_post_g[l], ffn1_w_gate=ffn1_w_gate[l],
            ffn1_w_up=ffn1_w_up[l], ffn1_w_down=ffn1_w_down[l],
            mix_pre_g=mix_pre_g[l], mix_post_g=mix_post_g[l], w_in=w_in[l], w_out=w_out[l],
            mlstm_b_i=mlstm_b_i[l].astype(f32), mlstm_b_f=mlstm_b_f[l].astype(f32),
            mlstm_norm_g=mlstm_norm_g[l].astype(f32),
            rwkv_mu=rwkv_mu[l].astype(f32), rwkv_w0=rwkv_w0[l].astype(f32),
            rwkv_w_up=rwkv_w_up[l].astype(f32), rwkv_a0=rwkv_a0[l].astype(f32),
            rwkv_a_up=rwkv_a_up[l].astype(f32), rwkv_g_up=rwkv_g_up[l].astype(f32),
            rwkv_k_k=rwkv_k_k[l].astype(f32), rwkv_k_a=rwkv_k_a[l].astype(f32),
            rwkv_r_k=rwkv_r_k[l].astype(f32), rwkv_ln_w=rwkv_ln_w[l].astype(f32),
            rwkv_ln_b=rwkv_ln_b[l].astype(f32),
            gla_alpha_up=gla_alpha_up[l].astype(f32), gla_alpha_b=gla_alpha_b[l].astype(f32),
            gla_norm_g=gla_norm_g[l].astype(f32),
            ffn2_pre_g=ffn2_pre_g[l], ffn2_post_g=ffn2_post_g[l], ffn2_w_gate=ffn2_w_gate[l],
            ffn2_w_up=ffn2_w_up[l], ffn2_w_down=ffn2_w_down[l],
        )
        yp, stp = layer(yp, zero_state, p)
        past = (state_mlstm_c[l].astype(f32), state_mlstm_n[l].astype(f32), state_mlstm_m[l].astype(f32),
                state_rwkv_s[l].astype(f32), state_rwkv_shift[l].astype(f32), state_gla_s[l].astype(f32))
        ys, sts = layer(ys, past, p)
        for j in range(6):
            new_p[j].append(stp[j])
            new_s[j].append(sts[j])
    dts = (state_mlstm_c.dtype, state_mlstm_n.dtype, state_mlstm_m.dtype,
           state_rwkv_s.dtype, state_rwkv_shift.dtype, state_gla_s.dtype)
    p_c, p_n, p_m, p_rs, p_rsh, p_gs = [jnp.stack(a).astype(dt) for a, dt in zip(new_p, dts)]
    s_c, s_n, s_m, s_rs, s_rsh, s_gs = [jnp.stack(a).astype(dt) for a, dt in zip(new_s, dts)]
    return (yp, ys, p_c, p_n, p_m, p_rs, p_rsh, p_gs, s_c, s_n, s_m, s_rs, s_rsh, s_gs)
```

```python
import functools
import math

import jax
import jax.numpy as jnp
from jax import lax
from jax.experimental import pallas as pl
from jax.experimental.pallas import tpu as pltpu

F32 = jnp.float32
BF16 = jnp.bfloat16

D_MODEL = 2048
D_FF = 5632
NORM_EPS = 1e-6
GN_EPS = 64e-5
GLA_TAU = 16.0
CHUNK = 64

H_M, DK_M, DV_M = 8, 64, 128
D_M = H_M * DV_M
H_R, N_R = 8, 64
D_R = H_R * N_R
W_LORA, A_LORA, G_LORA = 32, 32, 96
P_R = 3 * D_R + W_LORA + A_LORA + G_LORA
H_G, DK_G, DV_G = 4, 64, 128
D_G = H_G * DV_G
K_G = H_G * DK_G
ALPHA_LORA = 16
P_M = 2 * H_M * DK_M + 2 * D_M + 2 * H_M
P_G = 2 * K_G + 2 * D_G + ALPHA_LORA

C_QM, C_KM, C_VM, C_OM = 0, 512, 1024, 2048
C_RKV = 3072
C_GG, C_VG, C_QG, C_KG = 4608, 5120, 5632, 5888
C_LORA, LORA_W = 6144, 256
C_GATE, GATE_W = 6400, 128
C_XA, XA_W = 6528, 128
P_PAD = 6656

VMEM_LIMIT = 52 * 1024 * 1024
HI = lax.Precision.HIGHEST


def _dot(a, b):
    return jnp.dot(a.astype(BF16), b.astype(BF16), preferred_element_type=F32)


def _dot_nt(a, b):
    return lax.dot_general(a.astype(BF16), b.astype(BF16), (((1,), (1,)), ((), ())),
                           preferred_element_type=F32)


def _dot_tn(a, b):
    return lax.dot_general(a.astype(BF16), b.astype(BF16), (((0,), (0,)), ((), ())),
                           preferred_element_type=F32)


def _dot_hi(a, b):
    return jnp.dot(a, b, precision=HI, preferred_element_type=F32)


def _dot_nt_hi(a, b):
    return lax.dot_general(a, b, (((1,), (1,)), ((), ())), precision=HI,
                           preferred_element_type=F32)


def _rms(x, g):
    return x * lax.rsqrt(jnp.mean(x * x, axis=-1, keepdims=True) + NORM_EPS) * g


def _log_sigmoid(x):
    return jnp.minimum(x, 0.0) - jnp.log(1.0 + jnp.exp(-jnp.abs(x)))


def _softplus(x):
    return jnp.maximum(x, 0.0) + jnp.log(1.0 + jnp.exp(-jnp.abs(x)))


def _sigmoid(x):
    return 1.0 / (1.0 + jnp.exp(-x))


def _iota2(shape, dim):
    return lax.broadcasted_iota(jnp.int32, shape, dim)


def _tril_ones(n):
    return (_iota2((n, n), 1) <= _iota2((n, n), 0)).astype(F32)


def _ffn_body(nf, x_ref, pre_ref, post_ref, wg_ref, wu_ref, wd_ref, o_ref, h_ref):
    f = pl.program_id(1)

    @pl.when(f == 0)
    def _():
        h_ref[...] = _rms(x_ref[...], pre_ref[...]).astype(BF16)
        o_ref[...] = jnp.zeros_like(o_ref)

    h = h_ref[...]
    g = jnp.dot(h, wg_ref[...], preferred_element_type=F32)
    u = jnp.dot(h, wu_ref[...], preferred_element_type=F32)
    a = (g * _sigmoid(g) * u).astype(BF16)
    o_ref[...] += jnp.dot(a, wd_ref[...], preferred_element_type=F32)

    @pl.when(f == nf - 1)
    def _():
        o_ref[...] = x_ref[...] + 0.5 * _rms(o_ref[...], post_ref[...])


def _ffn(x, pre_g, post_g, wg, wu, wd, layer, tm, tf):
    m, d = x.shape
    nf = D_FF // tf
    return pl.pallas_call(
        functools.partial(_ffn_body, nf),
        grid=(m // tm, nf),
        in_specs=[
            pl.BlockSpec((tm, d), lambda i, f: (i, 0)),
            pl.BlockSpec((None, 1, d), lambda i, f: (layer, 0, 0)),
            pl.BlockSpec((None, 1, d), lambda i, f: (layer, 0, 0)),
            pl.BlockSpec((None, d, tf), lambda i, f: (layer, 0, f)),
            pl.BlockSpec((None, d, tf), lambda i, f: (layer, 0, f)),
            pl.BlockSpec((None, tf, d), lambda i, f: (layer, f, 0)),
        ],
        out_specs=pl.BlockSpec((tm, d), lambda i, f: (i, 0)),
        out_shape=jax.ShapeDtypeStruct((m, d), F32),
        scratch_shapes=[pltpu.VMEM((tm, d), BF16)],
        compiler_params=pltpu.CompilerParams(
            dimension_semantics=("arbitrary", "arbitrary"), vmem_limit_bytes=VMEM_LIMIT),
        name="ffn",
    )(x, pre_g, post_g, wg, wu, wd)


def _inproj_body(x_ref, g_ref, w_ref, o_ref, h_ref):
    @pl.when(pl.program_id(1) == 0)
    def _():
        h_ref[...] = _rms(x_ref[...], g_ref[...]).astype(BF16)

    o_ref[...] = jnp.dot(h_ref[...], w_ref[...], preferred_element_type=F32)


def _inproj(x, pre_g, w_in, layer, tm, tn):
    m, d = x.shape
    return pl.pallas_call(
        _inproj_body,
        grid=(m // tm, P_PAD // tn),
        in_specs=[
            pl.BlockSpec((tm, d), lambda i, j: (i, 0)),
            pl.BlockSpec((None, 1, d), lambda i, j: (layer, 0, 0)),
            pl.BlockSpec((None, d, tn), lambda i, j: (layer, 0, j)),
        ],
        out_specs=pl.BlockSpec((tm, tn), lambda i, j: (i, j)),
        out_shape=jax.ShapeDtypeStruct((m, P_PAD), F32),
        scratch_shapes=[pltpu.VMEM((tm, d), BF16)],
        compiler_params=pltpu.CompilerParams(
            dimension_semantics=("arbitrary", "arbitrary"), vmem_limit_bytes=VMEM_LIMIT),
        name="inproj",
    )(x, pre_g, w_in)


def _outproj_body(x_ref, ym_ref, yr_ref, yg_ref, w_ref, g_ref, o_ref):
    y = jnp.dot(ym_ref[...].astype(BF16), w_ref[0:D_M, :], preferred_element_type=F32)
    y += jnp.dot(yr_ref[...].astype(BF16), w_ref[D_M:D_M + D_R, :], preferred_element_type=F32)
    y += jnp.dot(yg_ref[...].astype(BF16), w_ref[D_M + D_R:, :], preferred_element_type=F32)
    o_ref[...] = x_ref[...] + _rms(y, g_ref[...])


def _outproj(x, ym, yr, yg, w_out, post_g, layer, tm):
    m, d = x.shape
    return pl.pallas_call(
        _outproj_body,
        grid=(m // tm,),
        in_specs=[
            pl.BlockSpec((tm, d), lambda i: (i, 0)),
            pl.BlockSpec((tm, D_M), lambda i: (i, 0)),
            pl.BlockSpec((tm, D_R), lambda i: (i, 0)),
            pl.BlockSpec((tm, D_G), lambda i: (i, 0)),
            pl.BlockSpec((None, d, d), lambda i: (layer, 0, 0)),
            pl.BlockSpec((None, 1, d), lambda i: (layer, 0, 0)),
        ],
        out_specs=pl.BlockSpec((tm, d), lambda i: (i, 0)),
        out_shape=jax.ShapeDtypeStruct((m, d), F32),
        compiler_params=pltpu.CompilerParams(
            dimension_semantics=("arbitrary",), vmem_limit_bytes=VMEM_LIMIT),
        name="outproj",
    )(x, ym, yr, yg, w_out, post_g)


def _mlstm_body(L, nc, q_ref, k_ref, v_ref, o_ref, gt_ref, c0_ref, n0_ref, m0_ref, bias_ref, ng_ref,
                y_ref, c_out, n_out, m_out, c_s, n_s, m_s):
    c = pl.program_id(1)

    @pl.when(c == 0)
    def _():
        c_s[...] = c0_ref[0]
        n_s[...] = n0_ref[0]
        m_s[...] = m0_ref[0]

    z = gt_ref[...] + bias_ref[...]
    lane = _iota2(z.shape, 1)
    is_f = (lane >= H_M) & (lane < 2 * H_M)
    lf = jnp.where(is_f, _log_sigmoid(z), 0.0)
    gates = jnp.where(lane < H_M, z, _dot_hi(_tril_ones(L), lf))
    eye = (_iota2((GATE_W, GATE_W), 0) == _iota2((GATE_W, GATE_W), 1)).astype(F32)
    gates_t = _dot_nt_hi(eye, gates)
    causal = _iota2((L, L), 1) <= _iota2((L, L), 0)

    for h in range(H_M):
        q = q_ref[:, h * DK_M:(h + 1) * DK_M]
        k = k_ref[:, h * DK_M:(h + 1) * DK_M] * (DK_M ** -0.5)
        v = v_ref[:, h * DV_M:(h + 1) * DV_M]
        li_c = gates[:, h:h + 1]
        b_c = gates[:, H_M + h:H_M + h + 1]
        li_r = gates_t[h:h + 1, :]
        b_r = gates_t[H_M + h:H_M + h + 1, :]
        m_prev = m_s[:, h:h + 1]
        C = c_s[h]
        n_row = n_s[h:h + 1, :]

        dmat = jnp.where(causal, b_c - b_r + li_r, -jnp.inf)
        inter = b_c + m_prev
        m_t = jnp.maximum(inter, jnp.max(dmat, axis=-1, keepdims=True))
        w_intra = jnp.exp(dmat - m_t) * _dot_nt(q, k)
        w_inter = jnp.exp(inter - m_t)
        num = w_inter * _dot(q, C) + _dot(w_intra, v)
        den = (w_inter * jnp.sum(q * n_row, axis=-1, keepdims=True)
               + jnp.sum(w_intra, axis=-1, keepdims=True))
        hh = num / jnp.maximum(jnp.abs(den), jnp.exp(-m_t))

        g_end = b_c[L - 1:L, :]
        dec = g_end - b_c + li_c
        m_new = jnp.maximum(g_end + m_prev, jnp.max(dec, axis=0, keepdims=True))
        kw = k * jnp.exp(dec - m_new)
        sc = jnp.exp(g_end + m_prev - m_new)
        c_s[h] = sc * C + _dot_tn(kw, v)
        n_s[h:h + 1, :] = sc * n_row + jnp.sum(kw, axis=0, keepdims=True)
        m_s[:, h:h + 1] = m_new

        sl = slice(h * DV_M, (h + 1) * DV_M)
        y_ref[:, sl] = _rms(hh, ng_ref[:, sl]) * _sigmoid(o_ref[:, sl])

    @pl.when(c == nc - 1)
    def _():
        c_out[0] = c_s[...]
        n_out[0] = n_s[...]
        m_out[0] = m_s[...]


def _mlstm(proj, row0, B, T, c0, n0, m0, bias, norm_g, layer):
    L = math.gcd(T, CHUNK)
    nc = T // L
    rb = row0 // L
    m0p = jnp.pad(m0, ((0, 0), (0, GATE_W - H_M))).reshape(B, 1, GATE_W)
    row = lambda w, j: pl.BlockSpec((L, w), lambda b, c: (rb + b * nc + c, j))
    y, c_new, n_new, m_new = pl.pallas_call(
        functools.partial(_mlstm_body, L, nc),
        grid=(B, nc),
        in_specs=[
            row(512, C_QM // 512), row(512, C_KM // 512), row(D_M, C_VM // D_M), row(D_M, C_OM // D_M),
            row(GATE_W, C_GATE // GATE_W),
            pl.BlockSpec((1, H_M, DK_M, DV_M), lambda b, c: (b, 0, 0, 0)),
            pl.BlockSpec((1, H_M, DK_M), lambda b, c: (b, 0, 0)),
            pl.BlockSpec((1, 1, GATE_W), lambda b, c: (b, 0, 0)),
            pl.BlockSpec((None, 1, GATE_W), lambda b, c: (layer, 0, 0)),
            pl.BlockSpec((None, 1, D_M), lambda b, c: (layer, 0, 0)),
        ],
        out_specs=[
            pl.BlockSpec((L, D_M), lambda b, c: (b * nc + c, 0)),
            pl.BlockSpec((1, H_M, DK_M, DV_M), lambda b, c: (b, 0, 0, 0)),
            pl.BlockSpec((1, H_M, DK_M), lambda b, c: (b, 0, 0)),
            pl.BlockSpec((1, 1, GATE_W), lambda b, c: (b, 0, 0)),
        ],
        out_shape=[
            jax.ShapeDtypeStruct((B * T, D_M), F32),
            jax.ShapeDtypeStruct((B, H_M, DK_M, DV_M), F32),
            jax.ShapeDtypeStruct((B, H_M, DK_M), F32),
            jax.ShapeDtypeStruct((B, 1, GATE_W), F32),
        ],
        scratch_shapes=[pltpu.VMEM((H_M, DK_M, DV_M), F32), pltpu.VMEM((H_M, DK_M), F32),
                        pltpu.VMEM((1, GATE_W), F32)],
        compiler_params=pltpu.CompilerParams(
            dimension_semantics=("arbitrary", "arbitrary"), vmem_limit_bytes=VMEM_LIMIT),
        name="mlstm",
    )(proj, proj, proj, proj, proj, c0, n0, m0p, bias, norm_g)
    return y, c_new, n_new, m_new[:, 0, :H_M]


def _seg_sum(x, e_ref):
    xh = x.astype(BF16)
    xl = (x - xh.astype(F32)).astype(BF16)
    e = e_ref[...]
    halves = []
    for j in range(x.shape[1] // 256):
        sl = slice(j * 256, (j + 1) * 256)
        halves.append(jnp.dot(xh[:, sl], e, preferred_element_type=F32)
                      + jnp.dot(xl[:, sl], e, preferred_element_type=F32))
    return jnp.concatenate(halves, axis=1)


def _rwkv_body(L, nc, rkv_ref, lo_ref, s0_ref, sh_rkv_ref, sh_lo_ref, mu_rkv_ref, mu_lo_ref,
               w0_ref, a0_ref, wup_ref, aup_ref, gup_ref, kk_ref, ka_ref, rk_ref, lnw_ref, lnb_ref,
               e_ref, diag_ref,
               y_ref, s_out, last_rkv, last_lo,
               s_s, car_rkv, car_lo, nkk_s, w_s, b_s, k_s, v_s, r_s, yy_s):
    c = pl.program_id(1)

    @pl.when(c == 0)
    def _():
        for h in range(H_R):
            s_s[:, h * N_R:(h + 1) * N_R] = s0_ref[0, h]
        car_rkv[...] = sh_rkv_ref[0]
        car_lo[...] = sh_lo_ref[0]

    def shift(x, car, mu):
        rolled = pltpu.roll(x, shift=1, axis=0)
        prev = jnp.where(_iota2(x.shape, 0) == 0, car, rolled)
        return x + (prev - x) * mu

    rkv = rkv_ref[...]
    lo = lo_ref[...]
    xs = shift(rkv, car_rkv[...], mu_rkv_ref[...])
    xl = shift(lo, car_lo[...], mu_lo_ref[...])
    car_rkv[...] = rkv[L - 1:L, :]
    car_lo[...] = lo[L - 1:L, :]

    r = xs[:, 0:D_R]
    k = xs[:, D_R:2 * D_R]
    v = xs[:, 2 * D_R:3 * D_R]
    w = -_softplus(-(w0_ref[...] + _dot(jnp.tanh(xl), wup_ref[...]))) - 0.5
    decay = jnp.exp(-jnp.exp(w))
    a = _sigmoid(a0_ref[...] + _dot(xl, aup_ref[...]))
    g = _dot(_sigmoid(xl), gup_ref[...])
    kk = k * kk_ref[...]
    kk = kk / jnp.maximum(jnp.sqrt(_seg_sum(kk * kk, e_ref)), 1e-12)
    k = k * (1.0 + (a - 1.0) * ka_ref[...])

    nkk_s[...] = -kk
    w_s[...] = decay
    b_s[...] = kk * a
    k_s[...] = k
    v_s[...] = v
    r_s[...] = r

    def step(t, carry):
        row = lambda ref: ref[pl.ds(t, 1), :]
        diag = diag_ref[...]
        S = s_s[...]
        sa = _seg_sum(S * row(nkk_s), e_ref)
        vexp = _seg_sum(diag * row(v_s), e_ref)
        S = S * row(w_s) + sa * row(b_s) + vexp * row(k_s)
        s_s[...] = S
        yexp = _seg_sum(S * row(r_s), e_ref)
        yy_s[pl.ds(t, 1), :] = jnp.sum(yexp * diag, axis=0, keepdims=True)
        return carry

    lax.fori_loop(0, L, step, 0)

    y = yy_s[...]
    mean = _seg_sum(y, e_ref) * (1.0 / N_R)
    d = y - mean
    var = _seg_sum(d * d, e_ref) * (1.0 / N_R)
    y = d * lax.rsqrt(var + GN_EPS) * lnw_ref[...] + lnb_ref[...]
    y = y + _seg_sum(r * k * rk_ref[...], e_ref) * v
    y_ref[...] = y * g

    @pl.when(c == nc - 1)
    def _():
        for h in range(H_R):
            s_out[0, h] = s_s[:, h * N_R:(h + 1) * N_R]
        last_rkv[0] = rkv[L - 1:L, :]
        last_lo[0] = lo[L - 1:L, :]


def _rwkv(proj, row0, B, T, s0, shift0, p, layer):
    L = math.gcd(T, CHUNK)
    nc = T // L
    rb = row0 // L
    sh_r, sh_xw, sh_k, sh_v, sh_xa, sh_xg = jnp.split(
        shift0, [D_R, D_R + W_LORA, 2 * D_R + W_LORA, 3 * D_R + W_LORA, 3 * D_R + W_LORA + A_LORA], axis=-1)
    sh_rkv = jnp.concatenate([sh_r, sh_k, sh_v], axis=-1).reshape(B, 1, 3 * D_R)
    sh_lo = jnp.concatenate(
        [sh_xw, sh_xa, sh_xg, jnp.zeros((B, LORA_W - W_LORA - A_LORA - G_LORA), F32)], axis=-1).reshape(B, 1, LORA_W)
    lane = jnp.arange(256)
    e256 = (lane[:, None] // N_R == lane[None, :] // N_R).astype(BF16)
    diag = (jnp.arange(N_R)[:, None] == (jnp.arange(D_R)[None, :] % N_R)).astype(F32)
    per_layer = lambda a: pl.BlockSpec((None,) + a.shape[1:], lambda b, c: (layer,) + (0,) * (a.ndim - 1))
    const = lambda a: pl.BlockSpec(a.shape, lambda b, c: (0,) * a.ndim)
    params = [p["mu_rkv"], p["mu_lo"], p["w0"], p["a0"], p["w_up"], p["a_up"], p["g_up"],
              p["k_k"], p["k_a"], p["r_k"], p["ln_w"], p["ln_b"]]
    y, s_new, last_rkv, last_lo = pl.pallas_call(
        functools.partial(_rwkv_body, L, nc),
        grid=(B, nc),
        in_specs=[
            pl.BlockSpec((L, 3 * D_R), lambda b, c: (rb + b * nc + c, C_RKV // (3 * D_R))),
            pl.BlockSpec((L, LORA_W), lambda b, c: (rb + b * nc + c, C_LORA // LORA_W)),
            pl.BlockSpec((1, H_R, N_R, N_R), lambda b, c: (b, 0, 0, 0)),
            pl.BlockSpec((1, 1, 3 * D_R), lambda b, c: (b, 0, 0)),
            pl.BlockSpec((1, 1, LORA_W), lambda b, c: (b, 0, 0)),
        ] + [per_layer(a) for a in params] + [const(e256), const(diag)],
        out_specs=[
            pl.BlockSpec((L, D_R), lambda b, c: (b * nc + c, 0)),
            pl.BlockSpec((1, H_R, N_R, N_R), lambda b, c: (b, 0, 0, 0)),
            pl.BlockSpec((1, 1, 3 * D_R), lambda b, c: (b, 0, 0)),
            pl.BlockSpec((1, 1, LORA_W), lambda b, c: (b, 0, 0)),
        ],
        out_shape=[
            jax.ShapeDtypeStruct((B * T, D_R), F32),
            jax.ShapeDtypeStruct((B, H_R, N_R, N_R), F32),
            jax.ShapeDtypeStruct((B, 1, 3 * D_R), F32),
            jax.ShapeDtypeStruct((B, 1, LORA_W), F32),
        ],
        scratch_shapes=[pltpu.VMEM((N_R, D_R), F32), pltpu.VMEM((1, 3 * D_R), F32), pltpu.VMEM((1, LORA_W), F32)]
        + [pltpu.VMEM((L, D_R), F32)] * 7,
        compiler_params=pltpu.CompilerParams(
            dimension_semantics=("arbitrary", "arbitrary"), vmem_limit_bytes=VMEM_LIMIT),
        name="rwkv",
    )(proj, proj, s0, sh_rkv, sh_lo, *params, e256, diag)
    last_rkv = last_rkv[:, 0]
    last_lo = last_lo[:, 0]
    shift_new = jnp.concatenate(
        [last_rkv[:, 0:D_R], last_lo[:, 0:W_LORA], last_rkv[:, D_R:3 * D_R],
         last_lo[:, W_LORA:W_LORA + A_LORA + G_LORA]], axis=-1)
    return y, s_new, shift_new


def _gla_body(L, nc, q_ref, k_ref, v_ref, gg_ref, xa_ref, s0_ref, aup_ref, ab_ref, ng_ref,
              y_ref, s_out, s_s, bc_s, o_s):
    c = pl.program_id(1)

    @pl.when(c == 0)
    def _():
        s_s[...] = s0_ref[0]

    la = _log_sigmoid(_dot(xa_ref[...], aup_ref[...]) + ab_ref[...]) * (1.0 / GLA_TAU)
    bc_s[...] = _dot_hi(_tril_ones(L), la)
    ls = min(L, 16)
    eye = (_iota2((DK_G, DK_G), 0) == _iota2((DK_G, DK_G), 1)).astype(F32)

    for h in range(H_G):
        ks = slice(h * DK_G, (h + 1) * DK_G)
        vs = slice(h * DV_G, (h + 1) * DV_G)
        q = q_ref[:, ks] * (DK_G ** -0.5)
        k = k_ref[:, ks]
        bc = bc_s[:, ks]
        S = s_s[h]
        o_s[...] = _dot(q * jnp.exp(bc), S)
        for i in range(L // ls):
            lo, hi = i * ls, (i + 1) * ls
            rho = bc_s[lo - 1:lo, ks] if i > 0 else jnp.zeros((1, DK_G), F32)
            qi = q[lo:hi] * jnp.exp(bc[lo:hi] - rho)
            ki = k[0:hi] * jnp.exp(rho - bc[0:hi])
            att = _dot_nt(qi, ki)
            att = jnp.where(_iota2(att.shape, 1) <= _iota2(att.shape, 0) + lo, att, 0.0)
            o_s[lo:hi, :] += _dot(att, v_ref[0:hi, vs])
        b_end = bc_s[L - 1:L, ks]
        kdec = k * jnp.exp(b_end - bc)
        s_s[h] = _dot_hi(eye * jnp.exp(b_end), S) + _dot_tn(kdec, v_ref[:, vs])
        gg = gg_ref[:, vs]
        y_ref[:, vs] = _rms(o_s[...], ng_ref[:, vs]) * (gg * _sigmoid(gg))

    @pl.when(c == nc - 1)
    def _():
        s_out[0] = s_s[...]


def _gla(proj, row0, B, T, s0, alpha_up, alpha_b, norm_g, layer):
    L = math.gcd(T, CHUNK)
    nc = T // L
    rb = row0 // L
    row = lambda w, j: pl.BlockSpec((L, w), lambda b, c: (rb + b * nc + c, j))
    y, s_new = pl.pallas_call(
        functools.partial(_gla_body, L, nc),
        grid=(B, nc),
        in_specs=[
            row(K_G, C_QG // K_G), row(K_G, C_KG // K_G), row(D_G, C_VG // D_G), row(D_G, C_GG // D_G),
            row(XA_W, C_XA // XA_W),
            pl.BlockSpec((1, H_G, DK_G, DV_G), lambda b, c: (b, 0, 0, 0)),
            pl.BlockSpec((None, XA_W, K_G), lambda b, c: (layer, 0, 0)),
            pl.BlockSpec((None, 1, K_G), lambda b, c: (layer, 0, 0)),
            pl.BlockSpec((None, 1, D_G), lambda b, c: (layer, 0, 0)),
        ],
        out_specs=[
            pl.BlockSpec((L, D_G), lambda b, c: (b * nc + c, 0)),
            pl.BlockSpec((1, H_G, DK_G, DV_G), lambda b, c: (b, 0, 0, 0)),
        ],
        out_shape=[
            jax.ShapeDtypeStruct((B * T, D_G), F32),
            jax.ShapeDtypeStruct((B, H_G, DK_G, DV_G), F32),
        ],
        scratch_shapes=[pltpu.VMEM((H_G, DK_G, DV_G), F32), pltpu.VMEM((L, K_G), F32),
                        pltpu.VMEM((L, DV_G), F32)],
        compiler_params=pltpu.CompilerParams(
            dimension_semantics=("arbitrary", "arbitrary"), vmem_limit_bytes=VMEM_LIMIT),
        name="gla",
    )(proj, proj, proj, proj, proj, s0, alpha_up, alpha_b, norm_g)
    return y, s_new


def _permute_cols(w_in):
    pm, pr, pg = jnp.split(w_in, [P_M, P_M + P_R], axis=-1)
    qm, km, vm, om, gates = jnp.split(pm, [512, 1024, 2048, 3072], axis=-1)
    r, xw, k, v, xa, xg = jnp.split(
        pr, [D_R, D_R + W_LORA, 2 * D_R + W_LORA, 3 * D_R + W_LORA, 3 * D_R + W_LORA + A_LORA], axis=-1)
    qg, kg, vg, xag, gg = jnp.split(pg, [K_G, 2 * K_G, 2 * K_G + D_G, 2 * K_G + D_G + ALPHA_LORA], axis=-1)
    zeros = lambda n: jnp.zeros(w_in.shape[:-1] + (n,), w_in.dtype)
    cols = [qm, km, vm, om, r, k, v, gg, vg, qg, kg,
            xw, xa, xg, zeros(LORA_W - W_LORA - A_LORA - G_LORA),
            gates, zeros(GATE_W - 2 * H_M), xag, zeros(XA_W - ALPHA_LORA)]
    return jnp.concatenate(cols, axis=-1)


def _rows_at(w, start, total):
    n = w.shape[1]
    return jnp.pad(w, ((0, 0), (start, total - start - n), (0, 0))).astype(BF16)


def _prep_rwkv(mu, w0, w_up, a0, a_up, g_up, k_k, k_a, r_k, ln_w, ln_b):
    depth = mu.shape[0]
    mu_r, mu_xw, mu_k, mu_v, mu_xa, mu_xg = jnp.split(
        mu, [D_R, D_R + W_LORA, 2 * D_R + W_LORA, 3 * D_R + W_LORA, 3 * D_R + W_LORA + A_LORA], axis=-1)
    row = lambda a: a.reshape(depth, 1, -1).astype(F32)
    return dict(
        mu_rkv=row(jnp.concatenate([mu_r, mu_k, mu_v], axis=-1)),
        mu_lo=row(jnp.concatenate(
            [mu_xw, mu_xa, mu_xg, jnp.zeros((depth, LORA_W - W_LORA - A_LORA - G_LORA), mu.dtype)], axis=-1)),
        w0=row(w0), a0=row(a0),
        w_up=_rows_at(w_up, 0, LORA_W),
        a_up=_rows_at(a_up, W_LORA, LORA_W),
        g_up=_rows_at(g_up, W_LORA + A_LORA, LORA_W),
        k_k=row(k_k), k_a=row(k_a), r_k=row(r_k), ln_w=row(ln_w), ln_b=row(ln_b),
    )


def kernel(x_prompt, x_sample, state_mlstm_c, state_mlstm_n, state_mlstm_m, state_rwkv_s, state_rwkv_shift, state_gla_s, ffn1_pre_g, ffn1_post_g, ffn1_w_gate, ffn1_w_up, ffn1_w_down, mix_pre_g, mix_post_g, w_in, w_out, mlstm_b_i, mlstm_b_f, mlstm_norm_g, rwkv_mu, rwkv_w0, rwkv_w_up, rwkv_a0, rwkv_a_up, rwkv_g_up, rwkv_k_k, rwkv_k_a, rwkv_r_k, rwkv_ln_w, rwkv_ln_b, gla_alpha_up, gla_alpha_b, gla_norm_g, ffn2_pre_g, ffn2_post_g, ffn2_w_gate, ffn2_w_up, ffn2_w_down):
    depth = w_in.shape[0]
    bp, tp, d = x_prompt.shape
    bs, ts, _ = x_sample.shape
    mp, ms = bp * tp, bs * ts
    m = mp + ms
    tm = 512
    assert m % tm == 0 and mp % math.gcd(tp, CHUNK) == 0 and mp % math.gcd(ts, CHUNK) == 0

    row = lambda a: a.reshape(depth, 1, -1).astype(F32)
    bf = lambda a: a.astype(BF16)
    f1 = (row(ffn1_pre_g), row(ffn1_post_g), bf(ffn1_w_gate), bf(ffn1_w_up), bf(ffn1_w_down))
    f2 = (row(ffn2_pre_g), row(ffn2_post_g), bf(ffn2_w_gate), bf(ffn2_w_up), bf(ffn2_w_down))
    w_in_p = _permute_cols(w_in).astype(BF16)
    w_out_b = bf(w_out)
    mix_pre, mix_post = row(mix_pre_g), row(mix_post_g)
    m_bias = row(jnp.concatenate(
        [mlstm_b_i, mlstm_b_f, jnp.zeros((depth, GATE_W - 2 * H_M), mlstm_b_i.dtype)], axis=-1))
    m_norm = row(mlstm_norm_g)
    rw = _prep_rwkv(rwkv_mu, rwkv_w0, rwkv_w_up, rwkv_a0, rwkv_a_up, rwkv_g_up, rwkv_k_k, rwkv_k_a,
                    rwkv_r_k, rwkv_ln_w, rwkv_ln_b)
    g_aup = _rows_at(gla_alpha_up, 0, XA_W)
    g_ab, g_norm = row(gla_alpha_b), row(gla_norm_g)

    zp = lambda *s: jnp.zeros((bp,) + s, F32)
    x = jnp.concatenate([x_prompt.reshape(mp, d), x_sample.reshape(ms, d)], axis=0)
    new_p = [[] for _ in range(6)]
    new_s = [[] for _ in range(6)]
    for l in range(depth):
        x = _ffn(x, *f1, l, tm, 512)
        proj = _inproj(x, mix_pre, w_in_p, l, tm, 512)
        past = (state_mlstm_c[l].astype(F32), state_mlstm_n[l].astype(F32), state_mlstm_m[l].astype(F32),
                state_rwkv_s[l].astype(F32), state_rwkv_shift[l].astype(F32), state_gla_s[l].astype(F32))
        zero = (zp(H_M, DK_M, DV_M), zp(H_M, DK_M), zp(H_M), zp(H_R, N_R, N_R), zp(P_R), zp(H_G, DK_G, DV_G))
        ys = []
        for (row0, B, T, st, acc) in ((0, bp, tp, zero, new_p), (mp, bs, ts, past, new_s)):
            ym, C, n, mm = _mlstm(proj, row0, B, T, st[0], st[1], st[2], m_bias, m_norm, l)
            yr, Sr, sh = _rwkv(proj, row0, B, T, st[3], st[4], rw, l)
            yg, Sg = _gla(proj, row0, B, T, st[5], g_aup, g_ab, g_norm, l)
            ys.append((ym, yr, yg))
            for j, a in enumerate((C, n, mm, Sr, sh, Sg)):
                acc[j].append(a)
        ym, yr, yg = (jnp.concatenate([a, b], axis=0) for a, b in zip(*ys))
        x = _outproj(x, ym, yr, yg, w_out_b, mix_post, l, tm)
        x = _ffn(x, *f2, l, tm, 512)

    dts = (state_mlstm_c.dtype, state_mlstm_n.dtype, state_mlstm_m.dtype,
           state_rwkv_s.dtype, state_rwkv_shift.dtype, state_gla_s.dtype)
    outs_p = [jnp.stack(a).astype(dt) for a, dt in zip(new_p, dts)]
    outs_s = [jnp.stack(a).astype(dt) for a, dt in zip(new_s, dts)]
    return (x[:mp].reshape(bp, tp, d), x[mp:].reshape(bs, ts, d), *outs_p, *outs_s)
```

```python
import functools
import math

import jax
import jax.numpy as jnp
from jax import lax
from jax.experimental import pallas as pl
from jax.experimental.pallas import tpu as pltpu

F32 = jnp.float32
BF16 = jnp.bfloat16

D_MODEL = 2048
D_FF = 5632
NORM_EPS = 1e-6
GN_EPS = 64e-5
GLA_TAU = 16.0
CHUNK = 64

H_M, DK_M, DV_M = 8, 64, 128
D_M = H_M * DV_M
H_R, N_R = 8, 64
D_R = H_R * N_R
W_LORA, A_LORA, G_LORA = 32, 32, 96
P_R = 3 * D_R + W_LORA + A_LORA + G_LORA
H_G, DK_G, DV_G = 4, 64, 128
D_G = H_G * DV_G
K_G = H_G * DK_G
ALPHA_LORA = 16
P_M = 2 * H_M * DK_M + 2 * D_M + 2 * H_M
P_G = 2 * K_G + 2 * D_G + ALPHA_LORA

C_QM, C_KM, C_VM, C_OM = 0, 512, 1024, 2048
C_RKV = 3072
C_GG, C_VG, C_QG, C_KG = 4608, 5120, 5632, 5888
C_LORA, LORA_W = 6144, 256
C_GATE, GATE_W = 6400, 128
C_XA, XA_W = 6528, 128
P_PAD = 6656

VMEM_LIMIT = 56 * 1024 * 1024
HI = lax.Precision.HIGHEST


def _dot(a, b):
    return jnp.dot(a.astype(BF16), b.astype(BF16), preferred_element_type=F32)


def _dot_nt(a, b):
    return lax.dot_general(a.astype(BF16), b.astype(BF16), (((1,), (1,)), ((), ())),
                           preferred_element_type=F32)


def _dot_tn(a, b):
    return lax.dot_general(a.astype(BF16), b.astype(BF16), (((0,), (0,)), ((), ())),
                           preferred_element_type=F32)


def _dot_hi(a, b):
    return jnp.dot(a, b, precision=HI, preferred_element_type=F32)


def _dot_nt_hi(a, b):
    return lax.dot_general(a, b, (((1,), (1,)), ((), ())), precision=HI,
                           preferred_element_type=F32)


def _rms(x, g):
    return x * lax.rsqrt(jnp.mean(x * x, axis=-1, keepdims=True) + NORM_EPS) * g


def _log_sigmoid(x):
    return jnp.minimum(x, 0.0) - jnp.log(1.0 + jnp.exp(-jnp.abs(x)))


def _softplus(x):
    return jnp.maximum(x, 0.0) + jnp.log(1.0 + jnp.exp(-jnp.abs(x)))


def _sigmoid(x):
    return 1.0 / (1.0 + jnp.exp(-x))


def _iota2(shape, dim):
    return lax.broadcasted_iota(jnp.int32, shape, dim)


def _tril_ones(n):
    return (_iota2((n, n), 1) <= _iota2((n, n), 0)).astype(F32)


def _ffn_body(nf, x_ref, pre_ref, post_ref, wg_ref, wu_ref, wd_ref, o_ref, h_ref):
    f = pl.program_id(1)

    @pl.when(f == 0)
    def _():
        h_ref[...] = _rms(x_ref[...], pre_ref[...]).astype(BF16)
        o_ref[...] = jnp.zeros_like(o_ref)

    h = h_ref[...]
    g = jnp.dot(h, wg_ref[...], preferred_element_type=F32)
    u = jnp.dot(h, wu_ref[...], preferred_element_type=F32)
    a = (g * _sigmoid(g) * u).astype(BF16)
    o_ref[...] += jnp.dot(a, wd_ref[...], preferred_element_type=F32)

    @pl.when(f == nf - 1)
    def _():
        o_ref[...] = x_ref[...] + 0.5 * _rms(o_ref[...], post_ref[...])


def _ffn(x, pre_g, post_g, wg, wu, wd, layer, tm, tf):
    m, d = x.shape
    nf = D_FF // tf
    return pl.pallas_call(
        functools.partial(_ffn_body, nf),
        grid=(m // tm, nf),
        in_specs=[
            pl.BlockSpec((tm, d), lambda i, f: (i, 0), pipeline_mode=pl.Buffered(1)),
            pl.BlockSpec((None, 1, d), lambda i, f: (layer, 0, 0)),
            pl.BlockSpec((None, 1, d), lambda i, f: (layer, 0, 0)),
            pl.BlockSpec((None, d, tf), lambda i, f: (layer, 0, f)),
            pl.BlockSpec((None, d, tf), lambda i, f: (layer, 0, f)),
            pl.BlockSpec((None, tf, d), lambda i, f: (layer, f, 0)),
        ],
        out_specs=pl.BlockSpec((tm, d), lambda i, f: (i, 0)),
        out_shape=jax.ShapeDtypeStruct((m, d), F32),
        scratch_shapes=[pltpu.VMEM((tm, d), BF16)],
        compiler_params=pltpu.CompilerParams(
            dimension_semantics=("arbitrary", "arbitrary"), vmem_limit_bytes=VMEM_LIMIT),
        name="ffn",
    )(x, pre_g, post_g, wg, wu, wd)


def _inproj_body(x_ref, g_ref, w_ref, o_ref, h_ref):
    @pl.when(pl.program_id(1) == 0)
    def _():
        h_ref[...] = _rms(x_ref[...], g_ref[...]).astype(BF16)

    o_ref[...] = jnp.dot(h_ref[...], w_ref[...], preferred_element_type=F32)


def _inproj(x, pre_g, w_in, layer, tm, tn):
    m, d = x.shape
    return pl.pallas_call(
        _inproj_body,
        grid=(m // tm, P_PAD // tn),
        in_specs=[
            pl.BlockSpec((tm, d), lambda i, j: (i, 0)),
            pl.BlockSpec((None, 1, d), lambda i, j: (layer, 0, 0)),
            pl.BlockSpec((None, d, tn), lambda i, j: (layer, 0, j)),
        ],
        out_specs=pl.BlockSpec((tm, tn), lambda i, j: (i, j)),
        out_shape=jax.ShapeDtypeStruct((m, P_PAD), F32),
        scratch_shapes=[pltpu.VMEM((tm, d), BF16)],
        compiler_params=pltpu.CompilerParams(
            dimension_semantics=("arbitrary", "arbitrary"), vmem_limit_bytes=VMEM_LIMIT),
        name="inproj",
    )(x, pre_g, w_in)


def _outproj_body(n_p, x_ref, ymp_ref, yrp_ref, ygp_ref, yms_ref, yrs_ref, ygs_ref, w_ref, g_ref, o_ref):
    def run(ym_ref, yr_ref, yg_ref):
        y = jnp.dot(ym_ref[...].astype(BF16), w_ref[0:D_M, :], preferred_element_type=F32)
        y += jnp.dot(yr_ref[...].astype(BF16), w_ref[D_M:D_M + D_R, :], preferred_element_type=F32)
        y += jnp.dot(yg_ref[...].astype(BF16), w_ref[D_M + D_R:, :], preferred_element_type=F32)
        o_ref[...] = x_ref[...] + _rms(y, g_ref[...])

    @pl.when(pl.program_id(0) < n_p)
    def _():
        run(ymp_ref, yrp_ref, ygp_ref)

    @pl.when(pl.program_id(0) >= n_p)
    def _():
        run(yms_ref, yrs_ref, ygs_ref)


def _outproj(x, y_p, y_s, w_out, post_g, layer, tm):
    m, d = x.shape
    n_p = y_p[0].shape[0] // tm
    assert y_p[0].shape[0] % tm == 0 and y_s[0].shape[0] % tm == 0
    pspec = lambda w: pl.BlockSpec((tm, w), lambda i: (jnp.minimum(i, n_p - 1), 0))
    sspec = lambda w: pl.BlockSpec((tm, w), lambda i: (jnp.maximum(i - n_p, 0), 0))
    return pl.pallas_call(
        functools.partial(_outproj_body, n_p),
        grid=(m // tm,),
        in_specs=[
            pl.BlockSpec((tm, d), lambda i: (i, 0)),
            pspec(D_M), pspec(D_R), pspec(D_G), sspec(D_M), sspec(D_R), sspec(D_G),
            pl.BlockSpec((None, d, d), lambda i: (layer, 0, 0)),
            pl.BlockSpec((None, 1, d), lambda i: (layer, 0, 0)),
        ],
        out_specs=pl.BlockSpec((tm, d), lambda i: (i, 0)),
        out_shape=jax.ShapeDtypeStruct((m, d), F32),
        compiler_params=pltpu.CompilerParams(
            dimension_semantics=("arbitrary",), vmem_limit_bytes=VMEM_LIMIT),
        name="outproj",
    )(x, *y_p, *y_s, w_out, post_g)


def _mlstm_body(L, nc, q_ref, k_ref, v_ref, o_ref, gt_ref, c0_ref, n0_ref, m0_ref, bias_ref, ng_ref,
                y_ref, c_out, n_out, m_out, c_s, n_s, m_s):
    c = pl.program_id(1)

    @pl.when(c == 0)
    def _():
        c_s[...] = c0_ref[0]
        n_s[...] = n0_ref[0]
        m_s[...] = m0_ref[0]

    z = gt_ref[...] + bias_ref[...]
    lane = _iota2(z.shape, 1)
    is_f = (lane >= H_M) & (lane < 2 * H_M)
    lf = jnp.where(is_f, _log_sigmoid(z), 0.0)
    gates = jnp.where(lane < H_M, z, _dot_hi(_tril_ones(L), lf))
    eye = (_iota2((GATE_W, GATE_W), 0) == _iota2((GATE_W, GATE_W), 1)).astype(F32)
    gates_t = _dot_nt_hi(eye, gates)
    causal = _iota2((L, L), 1) <= _iota2((L, L), 0)

    for h in range(H_M):
        q = q_ref[:, h * DK_M:(h + 1) * DK_M]
        k = k_ref[:, h * DK_M:(h + 1) * DK_M] * (DK_M ** -0.5)
        v = v_ref[:, h * DV_M:(h + 1) * DV_M]
        li_c = gates[:, h:h + 1]
        b_c = gates[:, H_M + h:H_M + h + 1]
        li_r = gates_t[h:h + 1, :]
        b_r = gates_t[H_M + h:H_M + h + 1, :]
        m_prev = m_s[:, h:h + 1]
        C = c_s[h]
        n_row = n_s[h:h + 1, :]

        dmat = jnp.where(causal, b_c - b_r + li_r, -jnp.inf)
        inter = b_c + m_prev
        m_t = jnp.maximum(inter, jnp.max(dmat, axis=-1, keepdims=True))
        w_intra = jnp.exp(dmat - m_t) * _dot_nt(q, k)
        w_inter = jnp.exp(inter - m_t)
        num = w_inter * _dot(q, C) + _dot(w_intra, v)
        den = (w_inter * jnp.sum(q * n_row, axis=-1, keepdims=True)
               + jnp.sum(w_intra, axis=-1, keepdims=True))
        hh = num / jnp.maximum(jnp.abs(den), jnp.exp(-m_t))

        g_end = b_c[L - 1:L, :]
        dec = g_end - b_c + li_c
        m_new = jnp.maximum(g_end + m_prev, jnp.max(dec, axis=0, keepdims=True))
        kw = k * jnp.exp(dec - m_new)
        sc = jnp.exp(g_end + m_prev - m_new)
        c_s[h] = sc * C + _dot_tn(kw, v)
        n_s[h:h + 1, :] = sc * n_row + jnp.sum(kw, axis=0, keepdims=True)
        m_s[:, h:h + 1] = m_new

        sl = slice(h * DV_M, (h + 1) * DV_M)
        y_ref[:, sl] = _rms(hh, ng_ref[:, sl]) * _sigmoid(o_ref[:, sl])

    @pl.when(c == nc - 1)
    def _():
        c_out[0] = c_s[...]
        n_out[0] = n_s[...]
        m_out[0] = m_s[...]


def _mlstm(proj, row0, B, T, c_all, n_all, m_all, sl, bias, norm_g, layer):
    L = math.gcd(T, CHUNK)
    nc = T // L
    rb = row0 // L
    row = lambda w, j: pl.BlockSpec((L, w), lambda b, c: (rb + b * nc + c, j))
    y, c_new, n_new, m_new = pl.pallas_call(
        functools.partial(_mlstm_body, L, nc),
        grid=(B, nc),
        in_specs=[
            row(512, C_QM // 512), row(512, C_KM // 512), row(D_M, C_VM // D_M), row(D_M, C_OM // D_M),
            row(GATE_W, C_GATE // GATE_W),
            pl.BlockSpec((None, 1, H_M, DK_M, DV_M), lambda b, c: (sl, b, 0, 0, 0)),
            pl.BlockSpec((None, 1, H_M, DK_M), lambda b, c: (sl, b, 0, 0)),
            pl.BlockSpec((None, 1, 1, GATE_W), lambda b, c: (sl, b, 0, 0)),
            pl.BlockSpec((None, 1, GATE_W), lambda b, c: (layer, 0, 0)),
            pl.BlockSpec((None, 1, D_M), lambda b, c: (layer, 0, 0)),
        ],
        out_specs=[
            pl.BlockSpec((L, D_M), lambda b, c: (b * nc + c, 0)),
            pl.BlockSpec((1, H_M, DK_M, DV_M), lambda b, c: (b, 0, 0, 0)),
            pl.BlockSpec((1, H_M, DK_M), lambda b, c: (b, 0, 0)),
            pl.BlockSpec((1, 1, GATE_W), lambda b, c: (b, 0, 0)),
        ],
        out_shape=[
            jax.ShapeDtypeStruct((B * T, D_M), F32),
            jax.ShapeDtypeStruct((B, H_M, DK_M, DV_M), F32),
            jax.ShapeDtypeStruct((B, H_M, DK_M), F32),
            jax.ShapeDtypeStruct((B, 1, GATE_W), F32),
        ],
        scratch_shapes=[pltpu.VMEM((H_M, DK_M, DV_M), F32), pltpu.VMEM((H_M, DK_M), F32),
                        pltpu.VMEM((1, GATE_W), F32)],
        compiler_params=pltpu.CompilerParams(
            dimension_semantics=("arbitrary", "arbitrary"), vmem_limit_bytes=VMEM_LIMIT),
        name="mlstm",
    )(proj, proj, proj, proj, proj, c_all, n_all, m_all, bias, norm_g)
    return y, c_new, n_new, m_new[:, 0, :H_M]


def _seg_sum(x, e_ref, passes):
    xh = x.astype(BF16)
    parts = [xh]
    if passes == 2:
        parts.append((x - xh.astype(F32)).astype(BF16))
    e = e_ref[...]
    halves = []
    for j in range(x.shape[1] // 256):
        sl = slice(j * 256, (j + 1) * 256)
        halves.append(sum(jnp.dot(p[:, sl], e, preferred_element_type=F32) for p in parts))
    return jnp.concatenate(halves, axis=1)


def _rwkv_body(L, nc, nch, n_src, *refs):
    rkv_refs, lo_refs = refs[:n_src], refs[n_src:2 * n_src]
    (s0_ref, sh_rkv_ref, sh_lo_ref, mu_rkv_ref, mu_lo_ref, w0_ref, a0_ref, wup_ref, aup_ref, gup_ref,
     kk_ref, ka_ref, rk_ref, lnw_ref, lnb_ref, e_ref, diag_ref,
     y_ref, s_out, last_rkv, last_lo,
     s_s, car_rkv, car_lo, nkk_s, w_s, b_s, k_s, v_s, r_s, yy_s) = refs[2 * n_src:]
    c = pl.program_id(1)

    @pl.when(c == 0)
    def _():
        for j in range(nch):
            for h in range(H_R):
                s_s[j, :, h * N_R:(h + 1) * N_R] = s0_ref[j, h]
        car_rkv[...] = sh_rkv_ref[...]
        car_lo[...] = sh_lo_ref[...]

    def slab(src):
        return src[0][...] if n_src == 1 else jnp.concatenate([r[...] for r in src], axis=0)

    def shift(x, car_ref, mu):
        rolled = pltpu.roll(x, shift=1, axis=0)
        car = jnp.concatenate(
            [jnp.broadcast_to(car_ref[j:j + 1, :], (L, x.shape[1])) for j in range(nch)], axis=0)
        prev = jnp.where(_iota2(x.shape, 0) % L == 0, car, rolled)
        return x + (prev - x) * mu

    rkv = slab(rkv_refs)
    lo = slab(lo_refs)
    xs = shift(rkv, car_rkv, mu_rkv_ref[...])
    xl = shift(lo, car_lo, mu_lo_ref[...])
    for j in range(nch):
        car_rkv[j:j + 1, :] = rkv[(j + 1) * L - 1:(j + 1) * L, :]
        car_lo[j:j + 1, :] = lo[(j + 1) * L - 1:(j + 1) * L, :]

    r = xs[:, 0:D_R]
    k = xs[:, D_R:2 * D_R]
    v = xs[:, 2 * D_R:3 * D_R]
    w = -_softplus(-(w0_ref[...] + _dot(jnp.tanh(xl), wup_ref[...]))) - 0.5
    decay = jnp.exp(-jnp.exp(w))
    a = _sigmoid(a0_ref[...] + _dot(xl, aup_ref[...]))
    g = _dot(_sigmoid(xl), gup_ref[...])
    kk = k * kk_ref[...]
    kk = kk / jnp.maximum(jnp.sqrt(_seg_sum(kk * kk, e_ref, 2)), 1e-12)
    k = k * (1.0 + (a - 1.0) * ka_ref[...])

    nkk_s[...] = -kk
    w_s[...] = decay
    b_s[...] = kk * a
    k_s[...] = k
    v_s[...] = v
    r_s[...] = r

    def step(t, carry):
        diag = diag_ref[...]
        for j in range(nch):
            row = lambda ref: ref[pl.ds(j * L + t, 1), :]
            S = s_s[j]
            sa = _seg_sum(S * row(nkk_s), e_ref, 2)
            vexp = _seg_sum(diag * row(v_s), e_ref, 1)
            S = S * row(w_s) + sa * row(b_s) + vexp * row(k_s)
            s_s[j] = S
            yexp = _seg_sum(S * row(r_s), e_ref, 1)
            yy_s[pl.ds(j * L + t, 1), :] = jnp.sum(yexp * diag, axis=0, keepdims=True)
        return carry

    lax.fori_loop(0, L, step, 0)

    y = yy_s[...]
    mean = _seg_sum(y, e_ref, 2) * (1.0 / N_R)
    d = y - mean
    var = _seg_sum(d * d, e_ref, 2) * (1.0 / N_R)
    y = d * lax.rsqrt(var + GN_EPS) * lnw_ref[...] + lnb_ref[...]
    y = (y + _seg_sum(r * k * rk_ref[...], e_ref, 2) * v) * g
    for j in range(nch):
        y_ref[j] = y[j * L:(j + 1) * L, :]

    @pl.when(c == nc - 1)
    def _():
        for j in range(nch):
            for h in range(H_R):
                s_out[j, h] = s_s[j, :, h * N_R:(h + 1) * N_R]
            last_rkv[j:j + 1, :] = rkv[(j + 1) * L - 1:(j + 1) * L, :]
            last_lo[j:j + 1, :] = lo[(j + 1) * L - 1:(j + 1) * L, :]


def _rwkv(proj, row0, B, T, nch, s_all, sl, shift0, p, layer):
    L = math.gcd(T, CHUNK)
    nc = T // L
    assert B % nch == 0
    sh_r, sh_xw, sh_k, sh_v, sh_xa, sh_xg = jnp.split(
        shift0, [D_R, D_R + W_LORA, 2 * D_R + W_LORA, 3 * D_R + W_LORA, 3 * D_R + W_LORA + A_LORA], axis=-1)
    sh_rkv = jnp.concatenate([sh_r, sh_k, sh_v], axis=-1)
    sh_lo = jnp.concatenate(
        [sh_xw, sh_xa, sh_xg, jnp.zeros((B, LORA_W - W_LORA - A_LORA - G_LORA), F32)], axis=-1)
    lane = jnp.arange(256)
    e256 = (lane[:, None] // N_R == lane[None, :] // N_R).astype(BF16)
    diag = (jnp.arange(N_R)[:, None] == (jnp.arange(D_R)[None, :] % N_R)).astype(F32)
    per_layer = lambda a: pl.BlockSpec((None,) + a.shape[1:], lambda g, c: (layer,) + (0,) * (a.ndim - 1))
    const = lambda a: pl.BlockSpec(a.shape, lambda g, c: (0,) * a.ndim)
    params = [p["mu_rkv"], p["mu_lo"], p["w0"], p["a0"], p["w_up"], p["a_up"], p["g_up"],
              p["k_k"], p["k_a"], p["r_k"], p["ln_w"], p["ln_b"]]
    if nc == 1:
        rows = nch * L
        assert row0 % rows == 0
        n_src = 1
        src = lambda w, col: [pl.BlockSpec((rows, w), lambda g, c: (row0 // rows + g, col // w))]
    else:
        n_src = nch
        src = lambda w, col: [
            pl.BlockSpec((L, w), lambda g, c, j=j: (row0 // L + (g * nch + j) * nc + c, col // w))
            for j in range(nch)]
    y, s_new, last_rkv, last_lo = pl.pallas_call(
        functools.partial(_rwkv_body, L, nc, nch, n_src),
        grid=(B // nch, nc),
        in_specs=src(3 * D_R, C_RKV) + src(LORA_W, C_LORA) + [
            pl.BlockSpec((None, nch, H_R, N_R, N_R), lambda g, c: (sl, g, 0, 0, 0)),
            pl.BlockSpec((nch, 3 * D_R), lambda g, c: (g, 0)),
            pl.BlockSpec((nch, LORA_W), lambda g, c: (g, 0)),
        ] + [per_layer(a) for a in params] + [const(e256), const(diag)],
        out_specs=[
            pl.BlockSpec((nch, L, D_R), lambda g, c: (g, c, 0)),
            pl.BlockSpec((nch, H_R, N_R, N_R), lambda g, c: (g, 0, 0, 0)),
            pl.BlockSpec((nch, 3 * D_R), lambda g, c: (g, 0)),
            pl.BlockSpec((nch, LORA_W), lambda g, c: (g, 0)),
        ],
        out_shape=[
            jax.ShapeDtypeStruct((B, T, D_R), F32),
            jax.ShapeDtypeStruct((B, H_R, N_R, N_R), F32),
            jax.ShapeDtypeStruct((B, 3 * D_R), F32),
            jax.ShapeDtypeStruct((B, LORA_W), F32),
        ],
        scratch_shapes=[pltpu.VMEM((nch, N_R, D_R), F32), pltpu.VMEM((nch, 3 * D_R), F32),
                        pltpu.VMEM((nch, LORA_W), F32)]
        + [pltpu.VMEM((nch * L, D_R), F32)] * 7,
        compiler_params=pltpu.CompilerParams(
            dimension_semantics=("arbitrary", "arbitrary"), vmem_limit_bytes=VMEM_LIMIT),
        name="rwkv",
    )(*([proj] * (2 * n_src)), s_all, sh_rkv, sh_lo, *params, e256, diag)
    shift_new = jnp.concatenate(
        [last_rkv[:, 0:D_R], last_lo[:, 0:W_LORA], last_rkv[:, D_R:3 * D_R],
         last_lo[:, W_LORA:W_LORA + A_LORA + G_LORA]], axis=-1)
    return y.reshape(B * T, D_R), s_new, shift_new


def _gla_body(L, nc, q_ref, k_ref, v_ref, gg_ref, xa_ref, s0_ref, aup_ref, ab_ref, ng_ref,
              y_ref, s_out, s_s, bc_s, o_s):
    c = pl.program_id(1)

    @pl.when(c == 0)
    def _():
        s_s[...] = s0_ref[0]

    la = _log_sigmoid(_dot(xa_ref[...], aup_ref[...]) + ab_ref[...]) * (1.0 / GLA_TAU)
    bc_s[...] = _dot_hi(_tril_ones(L), la)
    ls = min(L, 16)
    eye = (_iota2((DK_G, DK_G), 0) == _iota2((DK_G, DK_G), 1)).astype(F32)

    for h in range(H_G):
        ks = slice(h * DK_G, (h + 1) * DK_G)
        vs = slice(h * DV_G, (h + 1) * DV_G)
        q = q_ref[:, ks] * (DK_G ** -0.5)
        k = k_ref[:, ks]
        bc = bc_s[:, ks]
        S = s_s[h]
        o_s[...] = _dot(q * jnp.exp(bc), S)
        for i in range(L // ls):
            lo, hi = i * ls, (i + 1) * ls
            rho = bc_s[lo - 1:lo, ks] if i > 0 else jnp.zeros((1, DK_G), F32)
            qi = q[lo:hi] * jnp.exp(bc[lo:hi] - rho)
            ki = k[0:hi] * jnp.exp(rho - bc[0:hi])
            att = _dot_nt(qi, ki)
            att = jnp.where(_iota2(att.shape, 1) <= _iota2(att.shape, 0) + lo, att, 0.0)
            o_s[lo:hi, :] += _dot(att, v_ref[0:hi, vs])
        b_end = bc_s[L - 1:L, ks]
        kdec = k * jnp.exp(b_end - bc)
        s_s[h] = _dot_hi(eye * jnp.exp(b_end), S) + _dot_tn(kdec, v_ref[:, vs])
        gg = gg_ref[:, vs]
        y_ref[:, vs] = _rms(o_s[...], ng_ref[:, vs]) * (gg * _sigmoid(gg))

    @pl.when(c == nc - 1)
    def _():
        s_out[0] = s_s[...]


def _gla(proj, row0, B, T, s_all, sl, alpha_up, alpha_b, norm_g, layer):
    L = math.gcd(T, CHUNK)
    nc = T // L
    rb = row0 // L
    row = lambda w, j: pl.BlockSpec((L, w), lambda b, c: (rb + b * nc + c, j))
    y, s_new = pl.pallas_call(
        functools.partial(_gla_body, L, nc),
        grid=(B, nc),
        in_specs=[
            row(K_G, C_QG // K_G), row(K_G, C_KG // K_G), row(D_G, C_VG // D_G), row(D_G, C_GG // D_G),
            row(XA_W, C_XA // XA_W),
            pl.BlockSpec((None, 1, H_G, DK_G, DV_G), lambda b, c: (sl, b, 0, 0, 0)),
            pl.BlockSpec((None, XA_W, K_G), lambda b, c: (layer, 0, 0)),
            pl.BlockSpec((None, 1, K_G), lambda b, c: (layer, 0, 0)),
            pl.BlockSpec((None, 1, D_G), lambda b, c: (layer, 0, 0)),
        ],
        out_specs=[
            pl.BlockSpec((L, D_G), lambda b, c: (b * nc + c, 0)),
            pl.BlockSpec((1, H_G, DK_G, DV_G), lambda b, c: (b, 0, 0, 0)),
        ],
        out_shape=[
            jax.ShapeDtypeStruct((B * T, D_G), F32),
            jax.ShapeDtypeStruct((B, H_G, DK_G, DV_G), F32),
        ],
        scratch_shapes=[pltpu.VMEM((H_G, DK_G, DV_G), F32), pltpu.VMEM((L, K_G), F32),
                        pltpu.VMEM((L, DV_G), F32)],
        compiler_params=pltpu.CompilerParams(
            dimension_semantics=("arbitrary", "arbitrary"), vmem_limit_bytes=VMEM_LIMIT),
        name="gla",
    )(proj, proj, proj, proj, proj, s_all, alpha_up, alpha_b, norm_g)
    return y, s_new


def _permute_cols(w_in):
    pm, pr, pg = jnp.split(w_in, [P_M, P_M + P_R], axis=-1)
    qm, km, vm, om, gates = jnp.split(pm, [512, 1024, 2048, 3072], axis=-1)
    r, xw, k, v, xa, xg = jnp.split(
        pr, [D_R, D_R + W_LORA, 2 * D_R + W_LORA, 3 * D_R + W_LORA, 3 * D_R + W_LORA + A_LORA], axis=-1)
    qg, kg, vg, xag, gg = jnp.split(pg, [K_G, 2 * K_G, 2 * K_G + D_G, 2 * K_G + D_G + ALPHA_LORA], axis=-1)
    zeros = lambda n: jnp.zeros(w_in.shape[:-1] + (n,), w_in.dtype)
    cols = [qm, km, vm, om, r, k, v, gg, vg, qg, kg,
            xw, xa, xg, zeros(LORA_W - W_LORA - A_LORA - G_LORA),
            gates, zeros(GATE_W - 2 * H_M), xag, zeros(XA_W - ALPHA_LORA)]
    return jnp.concatenate(cols, axis=-1)


def _rows_at(w, start, total):
    n = w.shape[1]
    return jnp.pad(w, ((0, 0), (start, total - start - n), (0, 0))).astype(BF16)


def _prep_rwkv(mu, w0, w_up, a0, a_up, g_up, k_k, k_a, r_k, ln_w, ln_b):
    depth = mu.shape[0]
    mu_r, mu_xw, mu_k, mu_v, mu_xa, mu_xg = jnp.split(
        mu, [D_R, D_R + W_LORA, 2 * D_R + W_LORA, 3 * D_R + W_LORA, 3 * D_R + W_LORA + A_LORA], axis=-1)
    row = lambda a: a.reshape(depth, 1, -1).astype(F32)
    return dict(
        mu_rkv=row(jnp.concatenate([mu_r, mu_k, mu_v], axis=-1)),
        mu_lo=row(jnp.concatenate(
            [mu_xw, mu_xa, mu_xg, jnp.zeros((depth, LORA_W - W_LORA - A_LORA - G_LORA), mu.dtype)], axis=-1)),
        w0=row(w0), a0=row(a0),
        w_up=_rows_at(w_up, 0, LORA_W),
        a_up=_rows_at(a_up, W_LORA, LORA_W),
        g_up=_rows_at(g_up, W_LORA + A_LORA, LORA_W),
        k_k=row(k_k), k_a=row(k_a), r_k=row(r_k), ln_w=row(ln_w), ln_b=row(ln_b),
    )


def _pad_m(m):
    return jnp.pad(m.astype(F32), ((0, 0), (0, 0), (0, GATE_W - H_M)))[:, :, None, :]


def kernel(x_prompt, x_sample, state_mlstm_c, state_mlstm_n, state_mlstm_m, state_rwkv_s, state_rwkv_shift, state_gla_s, ffn1_pre_g, ffn1_post_g, ffn1_w_gate, ffn1_w_up, ffn1_w_down, mix_pre_g, mix_post_g, w_in, w_out, mlstm_b_i, mlstm_b_f, mlstm_norm_g, rwkv_mu, rwkv_w0, rwkv_w_up, rwkv_a0, rwkv_a_up, rwkv_g_up, rwkv_k_k, rwkv_k_a, rwkv_r_k, rwkv_ln_w, rwkv_ln_b, gla_alpha_up, gla_alpha_b, gla_norm_g, ffn2_pre_g, ffn2_post_g, ffn2_w_gate, ffn2_w_up, ffn2_w_down):
    depth = w_in.shape[0]
    bp, tp, d = x_prompt.shape
    bs, ts, _ = x_sample.shape
    mp, ms = bp * tp, bs * ts
    m = mp + ms
    tm_ffn, tm_proj, tm_out = 1024, 1024, 512
    assert m % tm_ffn == 0 and m % tm_proj == 0 and mp % tm_out == 0 and ms % tm_out == 0

    row = lambda a: a.reshape(depth, 1, -1).astype(F32)
    bf = lambda a: a.astype(BF16)
    f1 = (row(ffn1_pre_g), row(ffn1_post_g), bf(ffn1_w_gate), bf(ffn1_w_up), bf(ffn1_w_down))
    f2 = (row(ffn2_pre_g), row(ffn2_post_g), bf(ffn2_w_gate), bf(ffn2_w_up), bf(ffn2_w_down))
    w_in_p = _permute_cols(w_in).astype(BF16)
    w_out_b = bf(w_out)
    mix_pre, mix_post = row(mix_pre_g), row(mix_post_g)
    m_bias = row(jnp.concatenate(
        [mlstm_b_i, mlstm_b_f, jnp.zeros((depth, GATE_W - 2 * H_M), mlstm_b_i.dtype)], axis=-1))
    m_norm = row(mlstm_norm_g)
    rw = _prep_rwkv(rwkv_mu, rwkv_w0, rwkv_w_up, rwkv_a0, rwkv_a_up, rwkv_g_up, rwkv_k_k, rwkv_k_a,
                    rwkv_r_k, rwkv_ln_w, rwkv_ln_b)
    g_aup = _rows_at(gla_alpha_up, 0, XA_W)
    g_ab, g_norm = row(gla_alpha_b), row(gla_norm_g)

    zp = lambda *s: jnp.zeros((1, bp) + s, F32)
    st_p = (zp(H_M, DK_M, DV_M), zp(H_M, DK_M), _pad_m(zp(H_M)), zp(H_R, N_R, N_R), zp(P_R), zp(H_G, DK_G, DV_G))
    st_s = (state_mlstm_c.astype(F32), state_mlstm_n.astype(F32), _pad_m(state_mlstm_m),
            state_rwkv_s.astype(F32), state_rwkv_shift.astype(F32), state_gla_s.astype(F32))

    x = jnp.concatenate([x_prompt.reshape(mp, d), x_sample.reshape(ms, d)], axis=0)
    new_p = [[] for _ in range(6)]
    new_s = [[] for _ in range(6)]
    for l in range(depth):
        x = _ffn(x, *f1, l, tm_ffn, 512)
        proj = _inproj(x, mix_pre, w_in_p, l, tm_proj, 512)
        ys = []
        for (row0, B, T, nch, st, sl, acc) in ((0, bp, tp, bp, st_p, 0, new_p), (mp, bs, ts, 8, st_s, l, new_s)):
            ym, C, n, mm = _mlstm(proj, row0, B, T, st[0], st[1], st[2], sl, m_bias, m_norm, l)
            yr, Sr, sh = _rwkv(proj, row0, B, T, nch, st[3], sl, st[4][sl], rw, l)
            yg, Sg = _gla(proj, row0, B, T, st[5], sl, g_aup, g_ab, g_norm, l)
            ys.append((ym, yr, yg))
            for j, a in enumerate((C, n, mm, Sr, sh, Sg)):
                acc[j].append(a)
        x = _outproj(x, ys[0], ys[1], w_out_b, mix_post, l, tm_out)
        x = _ffn(x, *f2, l, tm_ffn, 512)

    dts = (state_mlstm_c.dtype, state_mlstm_n.dtype, state_mlstm_m.dtype,
           state_rwkv_s.dtype, state_rwkv_shift.dtype, state_gla_s.dtype)
    outs_p = [jnp.stack(a).astype(dt) for a, dt in zip(new_p, dts)]
    outs_s = [jnp.stack(a).astype(dt) for a, dt in zip(new_s, dts)]
    return (x[:mp].reshape(bp, tp, d), x[mp:].reshape(bs, ts, d), *outs_p, *outs_s)
```

```python
import functools
import math

import jax
import jax.numpy as jnp
from jax import lax
from jax.experimental import pallas as pl
from jax.experimental.pallas import tpu as pltpu

F32 = jnp.float32
BF16 = jnp.bfloat16

D_MODEL = 2048
D_FF = 5632
NORM_EPS = 1e-6
GN_EPS = 64e-5
GLA_TAU = 16.0
CHUNK = 64

H_M, DK_M, DV_M = 8, 64, 128
D_M = H_M * DV_M
H_R, N_R = 8, 64
D_R = H_R * N_R
W_LORA, A_LORA, G_LORA = 32, 32, 96
P_R = 3 * D_R + W_LORA + A_LORA + G_LORA
H_G, DK_G, DV_G = 4, 64, 128
D_G = H_G * DV_G
K_G = H_G * DK_G
ALPHA_LORA = 16
P_M = 2 * H_M * DK_M + 2 * D_M + 2 * H_M
P_G = 2 * K_G + 2 * D_G + ALPHA_LORA

C_QM, C_KM, C_VM, C_OM = 0, 512, 1024, 2048
C_RKV = 3072
C_GG, C_VG, C_QG, C_KG = 4608, 5120, 5632, 5888
C_LORA, LORA_W = 6144, 256
C_GATE, GATE_W = 6400, 128
C_XA, XA_W = 6528, 128
P_PAD = 6656

VMEM_LIMIT = 56 * 1024 * 1024
HI = lax.Precision.HIGHEST


def _dot(a, b):
    return jnp.dot(a.astype(BF16), b.astype(BF16), preferred_element_type=F32)


def _dot_nt(a, b):
    return lax.dot_general(a.astype(BF16), b.astype(BF16), (((1,), (1,)), ((), ())),
                           preferred_element_type=F32)


def _dot_tn(a, b):
    return lax.dot_general(a.astype(BF16), b.astype(BF16), (((0,), (0,)), ((), ())),
                           preferred_element_type=F32)


def _dot_hi(a, b):
    return jnp.dot(a, b, precision=HI, preferred_element_type=F32)


def _dot_nt_hi(a, b):
    return lax.dot_general(a, b, (((1,), (1,)), ((), ())), precision=HI,
                           preferred_element_type=F32)


def _rms(x, g):
    return x * lax.rsqrt(jnp.mean(x * x, axis=-1, keepdims=True) + NORM_EPS) * g


def _log_sigmoid(x):
    return jnp.minimum(x, 0.0) - jnp.log(1.0 + jnp.exp(-jnp.abs(x)))


def _softplus(x):
    return jnp.maximum(x, 0.0) + jnp.log(1.0 + jnp.exp(-jnp.abs(x)))


def _sigmoid(x):
    return 1.0 / (1.0 + jnp.exp(-x))


def _iota2(shape, dim):
    return lax.broadcasted_iota(jnp.int32, shape, dim)


def _tril_ones(n):
    return (_iota2((n, n), 1) <= _iota2((n, n), 0)).astype(F32)


def _ffn_body(nf, x_ref, pre_ref, post_ref, wg_ref, wu_ref, wd_ref, o_ref, h_ref):
    f = pl.program_id(1)

    @pl.when(f == 0)
    def _():
        h_ref[...] = _rms(x_ref[...], pre_ref[...]).astype(BF16)
        o_ref[...] = jnp.zeros_like(o_ref)

    h = h_ref[...]
    g = jnp.dot(h, wg_ref[...], preferred_element_type=F32)
    u = jnp.dot(h, wu_ref[...], preferred_element_type=F32)
    a = (g * _sigmoid(g) * u).astype(BF16)
    o_ref[...] += jnp.dot(a, wd_ref[...], preferred_element_type=F32)

    @pl.when(f == nf - 1)
    def _():
        o_ref[...] = x_ref[...] + 0.5 * _rms(o_ref[...], post_ref[...])


def _ffn(x, pre_g, post_g, wg, wu, wd, layer, tm, tf):
    m, d = x.shape
    nf = D_FF // tf
    return pl.pallas_call(
        functools.partial(_ffn_body, nf),
        grid=(m // tm, nf),
        in_specs=[
            pl.BlockSpec((tm, d), lambda i, f: (i, 0)),
            pl.BlockSpec((None, 1, d), lambda i, f: (layer, 0, 0)),
            pl.BlockSpec((None, 1, d), lambda i, f: (layer, 0, 0)),
            pl.BlockSpec((None, d, tf), lambda i, f: (layer, 0, f)),
            pl.BlockSpec((None, d, tf), lambda i, f: (layer, 0, f)),
            pl.BlockSpec((None, tf, d), lambda i, f: (layer, f, 0)),
        ],
        out_specs=pl.BlockSpec((tm, d), lambda i, f: (i, 0)),
        out_shape=jax.ShapeDtypeStruct((m, d), F32),
        scratch_shapes=[pltpu.VMEM((tm, d), BF16)],
        compiler_params=pltpu.CompilerParams(
            dimension_semantics=("arbitrary", "arbitrary"), vmem_limit_bytes=VMEM_LIMIT),
        name="ffn",
    )(x, pre_g, post_g, wg, wu, wd)


def _inproj_body(x_ref, g_ref, w_ref, o_ref, h_ref):
    @pl.when(pl.program_id(1) == 0)
    def _():
        h_ref[...] = _rms(x_ref[...], g_ref[...]).astype(BF16)

    o_ref[...] = jnp.dot(h_ref[...], w_ref[...], preferred_element_type=F32)


def _inproj(x, pre_g, w_in, layer, tm, tn):
    m, d = x.shape
    return pl.pallas_call(
        _inproj_body,
        grid=(m // tm, P_PAD // tn),
        in_specs=[
            pl.BlockSpec((tm, d), lambda i, j: (i, 0)),
            pl.BlockSpec((None, 1, d), lambda i, j: (layer, 0, 0)),
            pl.BlockSpec((None, d, tn), lambda i, j: (layer, 0, j)),
        ],
        out_specs=pl.BlockSpec((tm, tn), lambda i, j: (i, j)),
        out_shape=jax.ShapeDtypeStruct((m, P_PAD), F32),
        scratch_shapes=[pltpu.VMEM((tm, d), BF16)],
        compiler_params=pltpu.CompilerParams(
            dimension_semantics=("arbitrary", "arbitrary"), vmem_limit_bytes=VMEM_LIMIT),
        name="inproj",
    )(x, pre_g, w_in)


def _outproj_body(n_p, x_ref, ymp_ref, yrp_ref, ygp_ref, yms_ref, yrs_ref, ygs_ref, w_ref, g_ref, o_ref):
    def run(ym_ref, yr_ref, yg_ref):
        y = jnp.dot(ym_ref[...].astype(BF16), w_ref[0:D_M, :], preferred_element_type=F32)
        y += jnp.dot(yr_ref[...].astype(BF16), w_ref[D_M:D_M + D_R, :], preferred_element_type=F32)
        y += jnp.dot(yg_ref[...].astype(BF16), w_ref[D_M + D_R:, :], preferred_element_type=F32)
        o_ref[...] = x_ref[...] + _rms(y, g_ref[...])

    @pl.when(pl.program_id(0) < n_p)
    def _():
        run(ymp_ref, yrp_ref, ygp_ref)

    @pl.when(pl.program_id(0) >= n_p)
    def _():
        run(yms_ref, yrs_ref, ygs_ref)


def _outproj(x, y_p, y_s, w_out, post_g, layer, tm):
    m, d = x.shape
    n_p = y_p[0].shape[0] // tm
    assert y_p[0].shape[0] % tm == 0 and y_s[0].shape[0] % tm == 0
    pspec = lambda w: pl.BlockSpec((tm, w), lambda i: (jnp.minimum(i, n_p - 1), 0))
    sspec = lambda w: pl.BlockSpec((tm, w), lambda i: (jnp.maximum(i - n_p, 0), 0))
    return pl.pallas_call(
        functools.partial(_outproj_body, n_p),
        grid=(m // tm,),
        in_specs=[
            pl.BlockSpec((tm, d), lambda i: (i, 0)),
            pspec(D_M), pspec(D_R), pspec(D_G), sspec(D_M), sspec(D_R), sspec(D_G),
            pl.BlockSpec((None, d, d), lambda i: (layer, 0, 0)),
            pl.BlockSpec((None, 1, d), lambda i: (layer, 0, 0)),
        ],
        out_specs=pl.BlockSpec((tm, d), lambda i: (i, 0)),
        out_shape=jax.ShapeDtypeStruct((m, d), F32),
        compiler_params=pltpu.CompilerParams(
            dimension_semantics=("arbitrary",), vmem_limit_bytes=VMEM_LIMIT),
        name="outproj",
    )(x, *y_p, *y_s, w_out, post_g)


def _mlstm_body(L, nc, q_ref, k_ref, v_ref, o_ref, gt_ref, c0_ref, n0_ref, m0_ref, bias_ref, ng_ref,
                y_ref, c_out, n_out, m_out, c_s, n_s, m_s):
    c = pl.program_id(1)

    @pl.when(c == 0)
    def _():
        c_s[...] = c0_ref[0]
        n_s[...] = n0_ref[0]
        m_s[...] = m0_ref[0]

    z = gt_ref[...] + bias_ref[...]
    lane = _iota2(z.shape, 1)
    is_f = (lane >= H_M) & (lane < 2 * H_M)
    lf = jnp.where(is_f, _log_sigmoid(z), 0.0)
    gates = jnp.where(lane < H_M, z, _dot_hi(_tril_ones(L), lf))
    eye = (_iota2((GATE_W, GATE_W), 0) == _iota2((GATE_W, GATE_W), 1)).astype(F32)
    gates_t = _dot_nt_hi(eye, gates)
    causal = _iota2((L, L), 1) <= _iota2((L, L), 0)

    c_old = [c_s[h] for h in range(H_M)]
    n_old = n_s[...]
    m_old = m_s[...]
    c_new, y_new = [], []
    n_new = jnp.zeros_like(n_old)
    m_new_all = jnp.zeros_like(m_old)

    heads = range(H_M)
    q = [q_ref[:, h * DK_M:(h + 1) * DK_M] for h in heads]
    k = [k_ref[:, h * DK_M:(h + 1) * DK_M] * (DK_M ** -0.5) for h in heads]
    v = [v_ref[:, h * DV_M:(h + 1) * DV_M] for h in heads]
    qk = [_dot_nt(q[h], k[h]) for h in heads]
    qc = [_dot(q[h], c_old[h]) for h in heads]

    dexp, w_inter, m_t, kw, sc = [], [], [], [], []
    for h in heads:
        li_c = gates[:, h:h + 1]
        b_c = gates[:, H_M + h:H_M + h + 1]
        li_r = gates_t[h:h + 1, :]
        b_r = gates_t[H_M + h:H_M + h + 1, :]
        m_prev = m_old[:, h:h + 1]
        dmat = jnp.where(causal, b_c - b_r + li_r, -jnp.inf)
        inter = b_c + m_prev
        m_t.append(jnp.maximum(inter, jnp.max(dmat, axis=-1, keepdims=True)))
        dexp.append(jnp.exp(dmat - m_t[h]))
        w_inter.append(jnp.exp(inter - m_t[h]))
        g_end = b_c[L - 1:L, :]
        dec = g_end - b_c + li_c
        m_new = jnp.maximum(g_end + m_prev, jnp.max(dec, axis=0, keepdims=True))
        kw.append(k[h] * jnp.exp(dec - m_new))
        sc.append(jnp.exp(g_end + m_prev - m_new))
        m_new_all = jnp.where(_iota2(m_old.shape, 1) == h, m_new, m_new_all)

    kv = [_dot_tn(kw[h], v[h]) for h in heads]
    w_intra = [dexp[h] * qk[h] for h in heads]
    wv = [_dot(w_intra[h], v[h]) for h in heads]

    for h in heads:
        n_row = n_old[h:h + 1, :]
        num = w_inter[h] * qc[h] + wv[h]
        den = (w_inter[h] * jnp.sum(q[h] * n_row, axis=-1, keepdims=True)
               + jnp.sum(w_intra[h], axis=-1, keepdims=True))
        hh = num / jnp.maximum(jnp.abs(den), jnp.exp(-m_t[h]))
        c_new.append(sc[h] * c_old[h] + kv[h])
        n_new = jnp.where(_iota2(n_old.shape, 0) == h,
                          sc[h] * n_row + jnp.sum(kw[h], axis=0, keepdims=True), n_new)
        sl = slice(h * DV_M, (h + 1) * DV_M)
        y_new.append(_rms(hh, ng_ref[:, sl]) * _sigmoid(o_ref[:, sl]))

    for h in range(H_M):
        c_s[h] = c_new[h]
        y_ref[:, h * DV_M:(h + 1) * DV_M] = y_new[h]
    n_s[...] = n_new
    m_s[...] = m_new_all

    @pl.when(c == nc - 1)
    def _():
        c_out[0] = c_s[...]
        n_out[0] = n_s[...]
        m_out[0] = m_s[...]


def _mlstm(proj, row0, B, T, c_all, n_all, m_all, sl, bias, norm_g, layer):
    L = math.gcd(T, CHUNK)
    nc = T // L
    rb = row0 // L
    row = lambda w, j: pl.BlockSpec((L, w), lambda b, c: (rb + b * nc + c, j))
    y, c_new, n_new, m_new = pl.pallas_call(
        functools.partial(_mlstm_body, L, nc),
        grid=(B, nc),
        in_specs=[
            row(512, C_QM // 512), row(512, C_KM // 512), row(D_M, C_VM // D_M), row(D_M, C_OM // D_M),
            row(GATE_W, C_GATE // GATE_W),
            pl.BlockSpec((None, 1, H_M, DK_M, DV_M), lambda b, c: (sl, b, 0, 0, 0)),
            pl.BlockSpec((None, 1, H_M, DK_M), lambda b, c: (sl, b, 0, 0)),
            pl.BlockSpec((None, 1, 1, GATE_W), lambda b, c: (sl, b, 0, 0)),
            pl.BlockSpec((None, 1, GATE_W), lambda b, c: (layer, 0, 0)),
            pl.BlockSpec((None, 1, D_M), lambda b, c: (layer, 0, 0)),
        ],
        out_specs=[
            pl.BlockSpec((L, D_M), lambda b, c: (b * nc + c, 0)),
            pl.BlockSpec((1, H_M, DK_M, DV_M), lambda b, c: (b, 0, 0, 0)),
            pl.BlockSpec((1, H_M, DK_M), lambda b, c: (b, 0, 0)),
            pl.BlockSpec((1, 1, GATE_W), lambda b, c: (b, 0, 0)),
        ],
        out_shape=[
            jax.ShapeDtypeStruct((B * T, D_M), F32),
            jax.ShapeDtypeStruct((B, H_M, DK_M, DV_M), F32),
            jax.ShapeDtypeStruct((B, H_M, DK_M), F32),
            jax.ShapeDtypeStruct((B, 1, GATE_W), F32),
        ],
        scratch_shapes=[pltpu.VMEM((H_M, DK_M, DV_M), F32), pltpu.VMEM((H_M, DK_M), F32),
                        pltpu.VMEM((1, GATE_W), F32)],
        compiler_params=pltpu.CompilerParams(
            dimension_semantics=("arbitrary", "arbitrary"), vmem_limit_bytes=VMEM_LIMIT),
        name="mlstm",
    )(proj, proj, proj, proj, proj, c_all, n_all, m_all, bias, norm_g)
    return y, c_new, n_new, m_new[:, 0, :H_M]


def _seg_sum(x, e_ref, passes):
    xh = x.astype(BF16)
    parts = [xh]
    if passes == 2:
        parts.append((x - xh.astype(F32)).astype(BF16))
    e = e_ref[...]
    halves = []
    for j in range(x.shape[1] // 256):
        sl = slice(j * 256, (j + 1) * 256)
        halves.append(sum(jnp.dot(p[:, sl], e, preferred_element_type=F32) for p in parts))
    return jnp.concatenate(halves, axis=1)


def _rwkv_body(L, nc, nch, n_src, *refs):
    rkv_refs, lo_refs = refs[:n_src], refs[n_src:2 * n_src]
    (s0_ref, sh_rkv_ref, sh_lo_ref, mu_rkv_ref, mu_lo_ref, w0_ref, a0_ref, wup_ref, aup_ref, gup_ref,
     kk_ref, ka_ref, rk_ref, lnw_ref, lnb_ref, e_ref, diag_ref,
     y_ref, s_out, last_rkv, last_lo,
     s_s, car_rkv, car_lo, nkk_s, w_s, b_s, k_s, v_s, r_s, yy_s) = refs[2 * n_src:]
    c = pl.program_id(1)

    @pl.when(c == 0)
    def _():
        for j in range(nch):
            for h in range(H_R):
                s_s[j, :, h * N_R:(h + 1) * N_R] = s0_ref[j, h]
        car_rkv[...] = sh_rkv_ref[...]
        car_lo[...] = sh_lo_ref[...]

    def slab(src):
        return src[0][...] if n_src == 1 else jnp.concatenate([r[...] for r in src], axis=0)

    def shift(x, car_ref, mu):
        rolled = pltpu.roll(x, shift=1, axis=0)
        car = jnp.concatenate(
            [jnp.broadcast_to(car_ref[j:j + 1, :], (L, x.shape[1])) for j in range(nch)], axis=0)
        prev = jnp.where(_iota2(x.shape, 0) % L == 0, car, rolled)
        return x + (prev - x) * mu

    rkv = slab(rkv_refs)
    lo = slab(lo_refs)
    xs = shift(rkv, car_rkv, mu_rkv_ref[...])
    xl = shift(lo, car_lo, mu_lo_ref[...])
    for j in range(nch):
        car_rkv[j:j + 1, :] = rkv[(j + 1) * L - 1:(j + 1) * L, :]
        car_lo[j:j + 1, :] = lo[(j + 1) * L - 1:(j + 1) * L, :]

    r = xs[:, 0:D_R]
    k = xs[:, D_R:2 * D_R]
    v = xs[:, 2 * D_R:3 * D_R]
    w = -_softplus(-(w0_ref[...] + _dot(jnp.tanh(xl), wup_ref[...]))) - 0.5
    decay = jnp.exp(-jnp.exp(w))
    a = _sigmoid(a0_ref[...] + _dot(xl, aup_ref[...]))
    g = _dot(_sigmoid(xl), gup_ref[...])
    kk = k * kk_ref[...]
    kk = kk / jnp.maximum(jnp.sqrt(_seg_sum(kk * kk, e_ref, 2)), 1e-12)
    k = k * (1.0 + (a - 1.0) * ka_ref[...])

    nkk_s[...] = -kk
    w_s[...] = decay
    b_s[...] = kk * a
    k_s[...] = k
    v_s[...] = v
    r_s[...] = r

    diag = diag_ref[...]
    diag_b = diag.astype(BF16)
    e = e_ref[...]

    def head_sums(lhs):
        return jnp.concatenate(
            [jnp.dot(lhs[:, i * 256:(i + 1) * 256], e, preferred_element_type=F32)
             for i in range(D_R // 256)], axis=1)

    def put_y(j, t, yexp):
        yy_s[pl.ds(8 + j * L + t, 1), :] = jnp.sum(yexp * diag, axis=0, keepdims=True)

    def step(t, carry):
        tp = jnp.maximum(t - 1, 0)
        parts = []
        for j in range(nch):
            rowb = lambda ref, tt: ref[pl.ds(j * L + tt, 1), :].astype(BF16)
            sb = s_s[j].astype(BF16)
            parts += [sb * rowb(nkk_s, t), diag_b * rowb(v_s, t), sb * rowb(r_s, tp)]
        res = head_sums(jnp.concatenate(parts, axis=0))
        for j in range(nch):
            row = lambda ref: ref[pl.ds(j * L + t, 1), :]
            sa, vexp, yexp = (res[(3 * j + i) * N_R:(3 * j + i + 1) * N_R] for i in range(3))
            s_s[j] = s_s[j] * row(w_s) + sa * row(b_s) + vexp * row(k_s)
            put_y(j, t - 1, yexp)
        return carry

    lax.fori_loop(0, L, step, 0, unroll=2)
    last = jnp.concatenate(
        [s_s[j].astype(BF16) * r_s[(j + 1) * L - 1:(j + 1) * L, :].astype(BF16) for j in range(nch)], axis=0)
    res = head_sums(last)
    for j in range(nch):
        put_y(j, L - 1, res[j * N_R:(j + 1) * N_R])

    y = yy_s[8:, :]
    mean = _seg_sum(y, e_ref, 2) * (1.0 / N_R)
    d = y - mean
    var = _seg_sum(d * d, e_ref, 2) * (1.0 / N_R)
    y = d * lax.rsqrt(var + GN_EPS) * lnw_ref[...] + lnb_ref[...]
    y = (y + _seg_sum(r * k * rk_ref[...], e_ref, 2) * v) * g
    for j in range(nch):
        y_ref[j] = y[j * L:(j + 1) * L, :]

    @pl.when(c == nc - 1)
    def _():
        for j in range(nch):
            for h in range(H_R):
                s_out[j, h] = s_s[j, :, h * N_R:(h + 1) * N_R]
            last_rkv[j:j + 1, :] = rkv[(j + 1) * L - 1:(j + 1) * L, :]
            last_lo[j:j + 1, :] = lo[(j + 1) * L - 1:(j + 1) * L, :]


def _rwkv(proj, row0, B, T, nch, s_all, sl, shift0, p, layer):
    L = math.gcd(T, CHUNK)
    nc = T // L
    assert B % nch == 0
    sh_r, sh_xw, sh_k, sh_v, sh_xa, sh_xg = jnp.split(
        shift0, [D_R, D_R + W_LORA, 2 * D_R + W_LORA, 3 * D_R + W_LORA, 3 * D_R + W_LORA + A_LORA], axis=-1)
    sh_rkv = jnp.concatenate([sh_r, sh_k, sh_v], axis=-1)
    sh_lo = jnp.concatenate(
        [sh_xw, sh_xa, sh_xg, jnp.zeros((B, LORA_W - W_LORA - A_LORA - G_LORA), F32)], axis=-1)
    lane = jnp.arange(256)
    e256 = (lane[:, None] // N_R == lane[None, :] // N_R).astype(BF16)
    diag = (jnp.arange(N_R)[:, None] == (jnp.arange(D_R)[None, :] % N_R)).astype(F32)
    per_layer = lambda a: pl.BlockSpec((None,) + a.shape[1:], lambda g, c: (layer,) + (0,) * (a.ndim - 1))
    const = lambda a: pl.BlockSpec(a.shape, lambda g, c: (0,) * a.ndim)
    params = [p["mu_rkv"], p["mu_lo"], p["w0"], p["a0"], p["w_up"], p["a_up"], p["g_up"],
              p["k_k"], p["k_a"], p["r_k"], p["ln_w"], p["ln_b"]]
    if nc == 1:
        rows = nch * L
        assert row0 % rows == 0
        n_src = 1
        src = lambda w, col: [pl.BlockSpec((rows, w), lambda g, c: (row0 // rows + g, col // w))]
    else:
        n_src = nch
        src = lambda w, col: [
            pl.BlockSpec((L, w), lambda g, c, j=j: (row0 // L + (g * nch + j) * nc + c, col // w))
            for j in range(nch)]
    y, s_new, last_rkv, last_lo = pl.pallas_call(
        functools.partial(_rwkv_body, L, nc, nch, n_src),
        grid=(B // nch, nc),
        in_specs=src(3 * D_R, C_RKV) + src(LORA_W, C_LORA) + [
            pl.BlockSpec((None, nch, H_R, N_R, N_R), lambda g, c: (sl, g, 0, 0, 0)),
            pl.BlockSpec((nch, 3 * D_R), lambda g, c: (g, 0)),
            pl.BlockSpec((nch, LORA_W), lambda g, c: (g, 0)),
        ] + [per_layer(a) for a in params] + [const(e256), const(diag)],
        out_specs=[
            pl.BlockSpec((nch, L, D_R), lambda g, c: (g, c, 0)),
            pl.BlockSpec((nch, H_R, N_R, N_R), lambda g, c: (g, 0, 0, 0)),
            pl.BlockSpec((nch, 3 * D_R), lambda g, c: (g, 0)),
            pl.BlockSpec((nch, LORA_W), lambda g, c: (g, 0)),
        ],
        out_shape=[
            jax.ShapeDtypeStruct((B, T, D_R), F32),
            jax.ShapeDtypeStruct((B, H_R, N_R, N_R), F32),
            jax.ShapeDtypeStruct((B, 3 * D_R), F32),
            jax.ShapeDtypeStruct((B, LORA_W), F32),
        ],
        scratch_shapes=[pltpu.VMEM((nch, N_R, D_R), F32), pltpu.VMEM((nch, 3 * D_R), F32),
                        pltpu.VMEM((nch, LORA_W), F32)]
        + [pltpu.VMEM((nch * L, D_R), F32)] * 6 + [pltpu.VMEM((nch * L + 8, D_R), F32)],
        compiler_params=pltpu.CompilerParams(
            dimension_semantics=("arbitrary", "arbitrary"), vmem_limit_bytes=VMEM_LIMIT),
        name="rwkv",
    )(*([proj] * (2 * n_src)), s_all, sh_rkv, sh_lo, *params, e256, diag)
    shift_new = jnp.concatenate(
        [last_rkv[:, 0:D_R], last_lo[:, 0:W_LORA], last_rkv[:, D_R:3 * D_R],
         last_lo[:, W_LORA:W_LORA + A_LORA + G_LORA]], axis=-1)
    return y.reshape(B * T, D_R), s_new, shift_new


def _gla_body(L, nc, q_ref, k_ref, v_ref, gg_ref, xa_ref, s0_ref, aup_ref, ab_ref, ng_ref,
              y_ref, s_out, s_s, bc_s):
    c = pl.program_id(1)

    @pl.when(c == 0)
    def _():
        s_s[...] = s0_ref[0]

    la = _log_sigmoid(_dot(xa_ref[...], aup_ref[...]) + ab_ref[...]) * (1.0 / GLA_TAU)
    bc_s[...] = _dot_hi(_tril_ones(L), la)
    ls = min(L, 16)
    eye = (_iota2((DK_G, DK_G), 0) == _iota2((DK_G, DK_G), 1)).astype(F32)

    heads = range(H_G)
    ksl = [slice(h * DK_G, (h + 1) * DK_G) for h in heads]
    vsl = [slice(h * DV_G, (h + 1) * DV_G) for h in heads]
    s_old = [s_s[h] for h in heads]
    q = [q_ref[:, ksl[h]] * (DK_G ** -0.5) for h in heads]
    k = [k_ref[:, ksl[h]] for h in heads]
    bc = [bc_s[:, ksl[h]] for h in heads]
    b_end = [bc_s[L - 1:L, ksl[h]] for h in heads]
    inter = [_dot(q[h] * jnp.exp(bc[h]), s_old[h]) for h in heads]
    kv = [_dot_tn(k[h] * jnp.exp(b_end[h] - bc[h]), v_ref[:, vsl[h]]) for h in heads]
    sdec = [_dot_hi(eye * jnp.exp(b_end[h]), s_old[h]) for h in heads]
    nsub = L // ls
    att = [[None] * nsub for _ in heads]
    for i in range(nsub):
        lo, hi = i * ls, (i + 1) * ls
        for h in heads:
            rho = bc_s[lo - 1:lo, ksl[h]] if i > 0 else jnp.zeros((1, DK_G), F32)
            qi = q[h][lo:hi] * jnp.exp(bc[h][lo:hi] - rho)
            ki = k[h][0:hi] * jnp.exp(rho - bc[h][0:hi])
            a = _dot_nt(qi, ki)
            att[h][i] = jnp.where(_iota2(a.shape, 1) <= _iota2(a.shape, 0) + lo, a, 0.0)
    intra = [[_dot(att[h][i], v_ref[0:(i + 1) * ls, vsl[h]]) for i in range(nsub)] for h in heads]
    s_new, y_new = [], []
    for h in heads:
        o = inter[h] + jnp.concatenate(intra[h], axis=0)
        s_new.append(sdec[h] + kv[h])
        gg = gg_ref[:, vsl[h]]
        y_new.append(_rms(o, ng_ref[:, vsl[h]]) * (gg * _sigmoid(gg)))

    for h in range(H_G):
        s_s[h] = s_new[h]
        y_ref[:, h * DV_G:(h + 1) * DV_G] = y_new[h]

    @pl.when(c == nc - 1)
    def _():
        s_out[0] = s_s[...]


def _gla(proj, row0, B, T, s_all, sl, alpha_up, alpha_b, norm_g, layer):
    L = math.gcd(T, CHUNK)
    nc = T // L
    rb = row0 // L
    row = lambda w, j: pl.BlockSpec((L, w), lambda b, c: (rb + b * nc + c, j))
    y, s_new = pl.pallas_call(
        functools.partial(_gla_body, L, nc),
        grid=(B, nc),
        in_specs=[
            row(K_G, C_QG // K_G), row(K_G, C_KG // K_G), row(D_G, C_VG // D_G), row(D_G, C_GG // D_G),
            row(XA_W, C_XA // XA_W),
            pl.BlockSpec((None, 1, H_G, DK_G, DV_G), lambda b, c: (sl, b, 0, 0, 0)),
            pl.BlockSpec((None, XA_W, K_G), lambda b, c: (layer, 0, 0)),
            pl.BlockSpec((None, 1, K_G), lambda b, c: (layer, 0, 0)),
            pl.BlockSpec((None, 1, D_G), lambda b, c: (layer, 0, 0)),
        ],
        out_specs=[
            pl.BlockSpec((L, D_G), lambda b, c: (b * nc + c, 0)),
            pl.BlockSpec((1, H_G, DK_G, DV_G), lambda b, c: (b, 0, 0, 0)),
        ],
        out_shape=[
            jax.ShapeDtypeStruct((B * T, D_G), F32),
            jax.ShapeDtypeStruct((B, H_G, DK_G, DV_G), F32),
        ],
        scratch_shapes=[pltpu.VMEM((H_G, DK_G, DV_G), F32), pltpu.VMEM((L, K_G), F32)],
        compiler_params=pltpu.CompilerParams(
            dimension_semantics=("arbitrary", "arbitrary"), vmem_limit_bytes=VMEM_LIMIT),
        name="gla",
    )(proj, proj, proj, proj, proj, s_all, alpha_up, alpha_b, norm_g)
    return y, s_new


def _permute_cols(w_in):
    pm, pr, pg = jnp.split(w_in, [P_M, P_M + P_R], axis=-1)
    qm, km, vm, om, gates = jnp.split(pm, [512, 1024, 2048, 3072], axis=-1)
    r, xw, k, v, xa, xg = jnp.split(
        pr, [D_R, D_R + W_LORA, 2 * D_R + W_LORA, 3 * D_R + W_LORA, 3 * D_R + W_LORA + A_LORA], axis=-1)
    qg, kg, vg, xag, gg = jnp.split(pg, [K_G, 2 * K_G, 2 * K_G + D_G, 2 * K_G + D_G + ALPHA_LORA], axis=-1)
    zeros = lambda n: jnp.zeros(w_in.shape[:-1] + (n,), w_in.dtype)
    cols = [qm, km, vm, om, r, k, v, gg, vg, qg, kg,
            xw, xa, xg, zeros(LORA_W - W_LORA - A_LORA - G_LORA),
            gates, zeros(GATE_W - 2 * H_M), xag, zeros(XA_W - ALPHA_LORA)]
    return jnp.concatenate(cols, axis=-1)


def _rows_at(w, start, total):
    n = w.shape[1]
    return jnp.pad(w, ((0, 0), (start, total - start - n), (0, 0))).astype(BF16)


def _prep_rwkv(mu, w0, w_up, a0, a_up, g_up, k_k, k_a, r_k, ln_w, ln_b):
    depth = mu.shape[0]
    mu_r, mu_xw, mu_k, mu_v, mu_xa, mu_xg = jnp.split(
        mu, [D_R, D_R + W_LORA, 2 * D_R + W_LORA, 3 * D_R + W_LORA, 3 * D_R + W_LORA + A_LORA], axis=-1)
    row = lambda a: a.reshape(depth, 1, -1).astype(F32)
    return dict(
        mu_rkv=row(jnp.concatenate([mu_r, mu_k, mu_v], axis=-1)),
        mu_lo=row(jnp.concatenate(
            [mu_xw, mu_xa, mu_xg, jnp.zeros((depth, LORA_W - W_LORA - A_LORA - G_LORA), mu.dtype)], axis=-1)),
        w0=row(w0), a0=row(a0),
        w_up=_rows_at(w_up, 0, LORA_W),
        a_up=_rows_at(a_up, W_LORA, LORA_W),
        g_up=_rows_at(g_up, W_LORA + A_LORA, LORA_W),
        k_k=row(k_k), k_a=row(k_a), r_k=row(r_k), ln_w=row(ln_w), ln_b=row(ln_b),
    )


def _pad_m(m):
    return jnp.pad(m.astype(F32), ((0, 0), (0, 0), (0, GATE_W - H_M)))[:, :, None, :]


def kernel(x_prompt, x_sample, state_mlstm_c, state_mlstm_n, state_mlstm_m, state_rwkv_s, state_rwkv_shift, state_gla_s, ffn1_pre_g, ffn1_post_g, ffn1_w_gate, ffn1_w_up, ffn1_w_down, mix_pre_g, mix_post_g, w_in, w_out, mlstm_b_i, mlstm_b_f, mlstm_norm_g, rwkv_mu, rwkv_w0, rwkv_w_up, rwkv_a0, rwkv_a_up, rwkv_g_up, rwkv_k_k, rwkv_k_a, rwkv_r_k, rwkv_ln_w, rwkv_ln_b, gla_alpha_up, gla_alpha_b, gla_norm_g, ffn2_pre_g, ffn2_post_g, ffn2_w_gate, ffn2_w_up, ffn2_w_down):
    depth = w_in.shape[0]
    bp, tp, d = x_prompt.shape
    bs, ts, _ = x_sample.shape
    mp, ms = bp * tp, bs * ts
    m = mp + ms
    tm_ffn, tm_proj, tm_out = 512, 1024, 512
    assert m % tm_ffn == 0 and m % tm_proj == 0 and mp % tm_out == 0 and ms % tm_out == 0

    row = lambda a: a.reshape(depth, 1, -1).astype(F32)
    bf = lambda a: a.astype(BF16)
    f1 = (row(ffn1_pre_g), row(ffn1_post_g), bf(ffn1_w_gate), bf(ffn1_w_up), bf(ffn1_w_down))
    f2 = (row(ffn2_pre_g), row(ffn2_post_g), bf(ffn2_w_gate), bf(ffn2_w_up), bf(ffn2_w_down))
    w_in_p = _permute_cols(w_in).astype(BF16)
    w_out_b = bf(w_out)
    mix_pre, mix_post = row(mix_pre_g), row(mix_post_g)
    m_bias = row(jnp.concatenate(
        [mlstm_b_i, mlstm_b_f, jnp.zeros((depth, GATE_W - 2 * H_M), mlstm_b_i.dtype)], axis=-1))
    m_norm = row(mlstm_norm_g)
    rw = _prep_rwkv(rwkv_mu, rwkv_w0, rwkv_w_up, rwkv_a0, rwkv_a_up, rwkv_g_up, rwkv_k_k, rwkv_k_a,
                    rwkv_r_k, rwkv_ln_w, rwkv_ln_b)
    g_aup = _rows_at(gla_alpha_up, 0, XA_W)
    g_ab, g_norm = row(gla_alpha_b), row(gla_norm_g)

    zp = lambda *s: jnp.zeros((1, bp) + s, F32)
    st_p = (zp(H_M, DK_M, DV_M), zp(H_M, DK_M), _pad_m(zp(H_M)), zp(H_R, N_R, N_R), zp(P_R), zp(H_G, DK_G, DV_G))
    st_s = (state_mlstm_c.astype(F32), state_mlstm_n.astype(F32), _pad_m(state_mlstm_m),
            state_rwkv_s.astype(F32), state_rwkv_shift.astype(F32), state_gla_s.astype(F32))

    x = jnp.concatenate([x_prompt.reshape(mp, d), x_sample.reshape(ms, d)], axis=0)
    new_p = [[] for _ in range(6)]
    new_s = [[] for _ in range(6)]
    for l in range(depth):
        x = _ffn(x, *f1, l, tm_ffn, 512)
        proj = _inproj(x, mix_pre, w_in_p, l, tm_proj, 512)
        ys = []
        for (row0, B, T, nch, st, sl, acc) in ((0, bp, tp, bp, st_p, 0, new_p), (mp, bs, ts, 8, st_s, l, new_s)):
            ym, C, n, mm = _mlstm(proj, row0, B, T, st[0], st[1], st[2], sl, m_bias, m_norm, l)
            yr, Sr, sh = _rwkv(proj, row0, B, T, nch, st[3], sl, st[4][sl], rw, l)
            yg, Sg = _gla(proj, row0, B, T, st[5], sl, g_aup, g_ab, g_norm, l)
            ys.append((ym, yr, yg))
            for j, a in enumerate((C, n, mm, Sr, sh, Sg)):
                acc[j].append(a)
        x = _outproj(x, ys[0], ys[1], w_out_b, mix_post, l, tm_out)
        x = _ffn(x, *f2, l, tm_ffn, 512)

    dts = (state_mlstm_c.dtype, state_mlstm_n.dtype, state_mlstm_m.dtype,
           state_rwkv_s.dtype, state_rwkv_shift.dtype, state_gla_s.dtype)
    outs_p = [jnp.stack(a).astype(dt) for a, dt in zip(new_p, dts)]
    outs_s = [jnp.stack(a).astype(dt) for a, dt in zip(new_s, dts)]
    return (x[:mp].reshape(bp, tp, d), x[mp:].reshape(bs, ts, d), *outs_p, *outs_s)
```

```python
import functools
import math

import jax
import jax.numpy as jnp
from jax import lax
from jax.experimental import pallas as pl
from jax.experimental.pallas import tpu as pltpu

F32 = jnp.float32
BF16 = jnp.bfloat16

D_MODEL = 2048
D_FF = 5632
NORM_EPS = 1e-6
GN_EPS = 64e-5
GLA_TAU = 16.0
CHUNK = 64

H_M, DK_M, DV_M = 8, 64, 128
D_M = H_M * DV_M
H_R, N_R = 8, 64
D_R = H_R * N_R
W_LORA, A_LORA, G_LORA = 32, 32, 96
P_R = 3 * D_R + W_LORA + A_LORA + G_LORA
H_G, DK_G, DV_G = 4, 64, 128
D_G = H_G * DV_G
K_G = H_G * DK_G
ALPHA_LORA = 16
P_M = 2 * H_M * DK_M + 2 * D_M + 2 * H_M
P_G = 2 * K_G + 2 * D_G + ALPHA_LORA

C_QM, C_KM, C_VM, C_OM = 0, 512, 1024, 2048
C_RKV = 3072
C_GG, C_VG, C_QG, C_KG = 4608, 5120, 5632, 5888
C_LORA, LORA_W = 6144, 256
C_GATE, GATE_W = 6400, 128
C_XA, XA_W = 6528, 128
P_PAD = 6656

VMEM_LIMIT = 56 * 1024 * 1024
HI = lax.Precision.HIGHEST


def _dot(a, b):
    return jnp.dot(a.astype(BF16), b.astype(BF16), preferred_element_type=F32)


def _dot_nt(a, b):
    return lax.dot_general(a.astype(BF16), b.astype(BF16), (((1,), (1,)), ((), ())),
                           preferred_element_type=F32)


def _dot_tn(a, b):
    return lax.dot_general(a.astype(BF16), b.astype(BF16), (((0,), (0,)), ((), ())),
                           preferred_element_type=F32)


def _dot_hi(a, b):
    return jnp.dot(a, b, precision=HI, preferred_element_type=F32)


def _dot_nt_hi(a, b):
    return lax.dot_general(a, b, (((1,), (1,)), ((), ())), precision=HI,
                           preferred_element_type=F32)


def _rms(x, g):
    return x * lax.rsqrt(jnp.mean(x * x, axis=-1, keepdims=True) + NORM_EPS) * g


def _log_sigmoid(x):
    return jnp.minimum(x, 0.0) - jnp.log(1.0 + jnp.exp(-jnp.abs(x)))


def _softplus(x):
    return jnp.maximum(x, 0.0) + jnp.log(1.0 + jnp.exp(-jnp.abs(x)))


def _sigmoid(x):
    return 1.0 / (1.0 + jnp.exp(-x))


def _iota2(shape, dim):
    return lax.broadcasted_iota(jnp.int32, shape, dim)


def _tril_ones(n):
    return (_iota2((n, n), 1) <= _iota2((n, n), 0)).astype(F32)


def _ffn_body(nf, x_ref, pre_ref, post_ref, wg_ref, wu_ref, wd_ref, o_ref, h_ref):
    f = pl.program_id(1)

    @pl.when(f == 0)
    def _():
        h_ref[...] = _rms(x_ref[...], pre_ref[...]).astype(BF16)
        o_ref[...] = jnp.zeros_like(o_ref)

    h = h_ref[...]
    g = jnp.dot(h, wg_ref[...], preferred_element_type=F32)
    u = jnp.dot(h, wu_ref[...], preferred_element_type=F32)
    a = (g * _sigmoid(g) * u).astype(BF16)
    o_ref[...] += jnp.dot(a, wd_ref[...], preferred_element_type=F32)

    @pl.when(f == nf - 1)
    def _():
        o_ref[...] = x_ref[...] + 0.5 * _rms(o_ref[...], post_ref[...])


def _ffn(x, pre_g, post_g, wg, wu, wd, layer, tm, tf):
    m, d = x.shape
    nf = D_FF // tf
    return pl.pallas_call(
        functools.partial(_ffn_body, nf),
        grid=(m // tm, nf),
        in_specs=[
            pl.BlockSpec((tm, d), lambda i, f: (i, 0)),
            pl.BlockSpec((None, 1, d), lambda i, f: (layer, 0, 0)),
            pl.BlockSpec((None, 1, d), lambda i, f: (layer, 0, 0)),
            pl.BlockSpec((None, d, tf), lambda i, f: (layer, 0, f)),
            pl.BlockSpec((None, d, tf), lambda i, f: (layer, 0, f)),
            pl.BlockSpec((None, tf, d), lambda i, f: (layer, f, 0)),
        ],
        out_specs=pl.BlockSpec((tm, d), lambda i, f: (i, 0)),
        out_shape=jax.ShapeDtypeStruct((m, d), F32),
        scratch_shapes=[pltpu.VMEM((tm, d), BF16)],
        compiler_params=pltpu.CompilerParams(
            dimension_semantics=("arbitrary", "arbitrary"), vmem_limit_bytes=VMEM_LIMIT),
        name="ffn",
    )(x, pre_g, post_g, wg, wu, wd)


def _inproj_body(x_ref, g_ref, w_ref, o_ref, h_ref):
    @pl.when(pl.program_id(1) == 0)
    def _():
        h_ref[...] = _rms(x_ref[...], g_ref[...]).astype(BF16)

    o_ref[...] = jnp.dot(h_ref[...], w_ref[...], preferred_element_type=F32)


def _inproj(x, pre_g, w_in, layer, tm, tn):
    m, d = x.shape
    return pl.pallas_call(
        _inproj_body,
        grid=(m // tm, P_PAD // tn),
        in_specs=[
            pl.BlockSpec((tm, d), lambda i, j: (i, 0)),
            pl.BlockSpec((None, 1, d), lambda i, j: (layer, 0, 0)),
            pl.BlockSpec((None, d, tn), lambda i, j: (layer, 0, j)),
        ],
        out_specs=pl.BlockSpec((tm, tn), lambda i, j: (i, j)),
        out_shape=jax.ShapeDtypeStruct((m, P_PAD), F32),
        scratch_shapes=[pltpu.VMEM((tm, d), BF16)],
        compiler_params=pltpu.CompilerParams(
            dimension_semantics=("arbitrary", "arbitrary"), vmem_limit_bytes=VMEM_LIMIT),
        name="inproj",
    )(x, pre_g, w_in)


def _outproj_body(n_p, x_ref, ymp_ref, yrp_ref, ygp_ref, yms_ref, yrs_ref, ygs_ref, w_ref, g_ref, o_ref):
    def run(ym_ref, yr_ref, yg_ref):
        y = jnp.dot(ym_ref[...].astype(BF16), w_ref[0:D_M, :], preferred_element_type=F32)
        y += jnp.dot(yr_ref[...].astype(BF16), w_ref[D_M:D_M + D_R, :], preferred_element_type=F32)
        y += jnp.dot(yg_ref[...].astype(BF16), w_ref[D_M + D_R:, :], preferred_element_type=F32)
        o_ref[...] = x_ref[...] + _rms(y, g_ref[...])

    @pl.when(pl.program_id(0) < n_p)
    def _():
        run(ymp_ref, yrp_ref, ygp_ref)

    @pl.when(pl.program_id(0) >= n_p)
    def _():
        run(yms_ref, yrs_ref, ygs_ref)


def _outproj(x, y_p, y_s, w_out, post_g, layer, tm):
    m, d = x.shape
    n_p = y_p[0].shape[0] // tm
    assert y_p[0].shape[0] % tm == 0 and y_s[0].shape[0] % tm == 0
    pspec = lambda w: pl.BlockSpec((tm, w), lambda i: (jnp.minimum(i, n_p - 1), 0))
    sspec = lambda w: pl.BlockSpec((tm, w), lambda i: (jnp.maximum(i - n_p, 0), 0))
    return pl.pallas_call(
        functools.partial(_outproj_body, n_p),
        grid=(m // tm,),
        in_specs=[
            pl.BlockSpec((tm, d), lambda i: (i, 0)),
            pspec(D_M), pspec(D_R), pspec(D_G), sspec(D_M), sspec(D_R), sspec(D_G),
            pl.BlockSpec((None, d, d), lambda i: (layer, 0, 0)),
            pl.BlockSpec((None, 1, d), lambda i: (layer, 0, 0)),
        ],
        out_specs=pl.BlockSpec((tm, d), lambda i: (i, 0)),
        out_shape=jax.ShapeDtypeStruct((m, d), F32),
        compiler_params=pltpu.CompilerParams(
            dimension_semantics=("arbitrary",), vmem_limit_bytes=VMEM_LIMIT),
        name="outproj",
    )(x, *y_p, *y_s, w_out, post_g)


def _mlstm_body(L, nc, nb, q_ref, k_ref, v_ref, o_ref, gt_ref, c0_ref, n0_ref, m0_ref, bias_ref, ng_ref,
                y_ref, c_out, n_out, m_out, c_s, n_s, m_s):
    c = pl.program_id(1)

    @pl.when(c == 0)
    def _():
        c_s[...] = c0_ref[...]
        n_s[...] = n0_ref[...]
        m_s[...] = m0_ref[:, 0, :]

    eye = (_iota2((GATE_W, GATE_W), 0) == _iota2((GATE_W, GATE_W), 1)).astype(F32)
    causal = _iota2((L, L), 1) <= _iota2((L, L), 0)
    tril = _tril_ones(L)
    seqs = range(nb)
    rows = [slice(i * L, (i + 1) * L) for i in seqs]

    gates, gates_t = [], []
    for i in seqs:
        z = gt_ref[rows[i], :] + bias_ref[...]
        lane = _iota2(z.shape, 1)
        is_f = (lane >= H_M) & (lane < 2 * H_M)
        lf = jnp.where(is_f, _log_sigmoid(z), 0.0)
        gates.append(jnp.where(lane < H_M, z, _dot_hi(tril, lf)))
        gates_t.append(_dot_nt_hi(eye, gates[i]))

    units = [(i, h) for i in seqs for h in range(H_M)]
    c_old = [c_s[i, h] for i, h in units]
    n_old = [n_s[i] for i in seqs]
    m_old = [m_s[i:i + 1, :] for i in seqs]
    q = [q_ref[rows[i], h * DK_M:(h + 1) * DK_M] for i, h in units]
    k = [k_ref[rows[i], h * DK_M:(h + 1) * DK_M] * (DK_M ** -0.5) for i, h in units]
    v = [v_ref[rows[i], h * DV_M:(h + 1) * DV_M] for i, h in units]
    qk = [_dot_nt(q[u], k[u]) for u in range(len(units))]
    qc = [_dot(q[u], c_old[u]) for u in range(len(units))]

    dexp, w_inter, m_t, kw, sc = [], [], [], [], []
    m_new_all = [jnp.zeros_like(m_old[i]) for i in seqs]
    for u, (i, h) in enumerate(units):
        li_c = gates[i][:, h:h + 1]
        b_c = gates[i][:, H_M + h:H_M + h + 1]
        li_r = gates_t[i][h:h + 1, :]
        b_r = gates_t[i][H_M + h:H_M + h + 1, :]
        m_prev = m_old[i][:, h:h + 1]
        dmat = jnp.where(causal, b_c - b_r + li_r, -jnp.inf)
        inter = b_c + m_prev
        m_t.append(jnp.maximum(inter, jnp.max(dmat, axis=-1, keepdims=True)))
        dexp.append(jnp.exp(dmat - m_t[u]))
        w_inter.append(jnp.exp(inter - m_t[u]))
        g_end = b_c[L - 1:L, :]
        dec = g_end - b_c + li_c
        m_new = jnp.maximum(g_end + m_prev, jnp.max(dec, axis=0, keepdims=True))
        kw.append(k[u] * jnp.exp(dec - m_new))
        sc.append(jnp.exp(g_end + m_prev - m_new))
        m_new_all[i] = jnp.where(_iota2(m_old[i].shape, 1) == h, m_new, m_new_all[i])

    us = range(len(units))
    kv = [_dot_tn(kw[u], v[u]) for u in us]
    w_intra = [dexp[u] * qk[u] for u in us]
    ones_v = jnp.ones((L, DV_M), F32)
    wv = [_dot(w_intra[u], jnp.concatenate([v[u], ones_v], axis=1)) for u in us]
    n_rows = [n_old[i][h:h + 1, :] for i, h in units]
    qn = [_dot_nt(q[u], jnp.broadcast_to(n_rows[u], (DV_M, DK_M))) for u in us]
    hh = []
    for u in us:
        num = w_inter[u] * qc[u] + wv[u][:, :DV_M]
        den = w_inter[u] * qn[u] + wv[u][:, DV_M:]
        hh.append(num / jnp.maximum(jnp.abs(den), jnp.exp(-m_t[u])))
    ones_m = jnp.ones((DV_M, DV_M), F32)
    ms = [_dot(hh[u] * hh[u], ones_m) * (1.0 / DV_M) for u in us]

    c_new, y_new = [], []
    n_new = [jnp.zeros_like(n_old[i]) for i in seqs]
    for u, (i, h) in enumerate(units):
        c_new.append(sc[u] * c_old[u] + kv[u])
        n_new[i] = jnp.where(_iota2(n_old[i].shape, 0) == h,
                             sc[u] * n_rows[u] + jnp.sum(kw[u], axis=0, keepdims=True), n_new[i])
        sl = slice(h * DV_M, (h + 1) * DV_M)
        y_new.append(hh[u] * lax.rsqrt(ms[u] + NORM_EPS) * ng_ref[:, sl] * _sigmoid(o_ref[rows[i], sl]))

    for u, (i, h) in enumerate(units):
        c_s[i, h] = c_new[u]
        y_ref[rows[i], h * DV_M:(h + 1) * DV_M] = y_new[u]
    for i in seqs:
        n_s[i] = n_new[i]
        m_s[i:i + 1, :] = m_new_all[i]

    @pl.when(c == nc - 1)
    def _():
        c_out[...] = c_s[...]
        n_out[...] = n_s[...]
        m_out[:, 0, :] = m_s[...]


def _mlstm(proj, row0, B, T, nb, c_all, n_all, m_all, sl, bias, norm_g, layer):
    L = math.gcd(T, CHUNK)
    nc = T // L
    R = nb * L
    assert (nb == 1 or nc == 1) and B % nb == 0 and row0 % R == 0
    rb = row0 // R
    row = lambda w, j: pl.BlockSpec((R, w), lambda g, c: (rb + g * nc + c, j))
    y, c_new, n_new, m_new = pl.pallas_call(
        functools.partial(_mlstm_body, L, nc, nb),
        grid=(B // nb, nc),
        in_specs=[
            row(512, C_QM // 512), row(512, C_KM // 512), row(D_M, C_VM // D_M), row(D_M, C_OM // D_M),
            row(GATE_W, C_GATE // GATE_W),
            pl.BlockSpec((None, nb, H_M, DK_M, DV_M), lambda g, c: (sl, g, 0, 0, 0)),
            pl.BlockSpec((None, nb, H_M, DK_M), lambda g, c: (sl, g, 0, 0)),
            pl.BlockSpec((None, nb, 1, GATE_W), lambda g, c: (sl, g, 0, 0)),
            pl.BlockSpec((None, 1, GATE_W), lambda g, c: (layer, 0, 0)),
            pl.BlockSpec((None, 1, D_M), lambda g, c: (layer, 0, 0)),
        ],
        out_specs=[
            pl.BlockSpec((R, D_M), lambda g, c: (g * nc + c, 0)),
            pl.BlockSpec((nb, H_M, DK_M, DV_M), lambda g, c: (g, 0, 0, 0)),
            pl.BlockSpec((nb, H_M, DK_M), lambda g, c: (g, 0, 0)),
            pl.BlockSpec((nb, 1, GATE_W), lambda g, c: (g, 0, 0)),
        ],
        out_shape=[
            jax.ShapeDtypeStruct((B * T, D_M), F32),
            jax.ShapeDtypeStruct((B, H_M, DK_M, DV_M), F32),
            jax.ShapeDtypeStruct((B, H_M, DK_M), F32),
            jax.ShapeDtypeStruct((B, 1, GATE_W), F32),
        ],
        scratch_shapes=[pltpu.VMEM((nb, H_M, DK_M, DV_M), F32), pltpu.VMEM((nb, H_M, DK_M), F32),
                        pltpu.VMEM((nb, GATE_W), F32)],
        compiler_params=pltpu.CompilerParams(
            dimension_semantics=("arbitrary", "arbitrary"), vmem_limit_bytes=VMEM_LIMIT),
        name="mlstm",
    )(proj, proj, proj, proj, proj, c_all, n_all, m_all, bias, norm_g)
    return y, c_new, n_new, m_new[:, 0, :H_M]


def _seg_sum(x, e_ref, passes):
    xh = x.astype(BF16)
    parts = [xh]
    if passes == 2:
        parts.append((x - xh.astype(F32)).astype(BF16))
    e = e_ref[...]
    halves = []
    for j in range(x.shape[1] // 256):
        sl = slice(j * 256, (j + 1) * 256)
        halves.append(sum(jnp.dot(p[:, sl], e, preferred_element_type=F32) for p in parts))
    return jnp.concatenate(halves, axis=1)


def _rwkv_body(L, nc, nch, n_src, *refs):
    rkv_refs, lo_refs = refs[:n_src], refs[n_src:2 * n_src]
    (s0_ref, sh_rkv_ref, sh_lo_ref, mu_rkv_ref, mu_lo_ref, w0_ref, a0_ref, wup_ref, aup_ref, gup_ref,
     kk_ref, ka_ref, rk_ref, lnw_ref, lnb_ref, e_ref, diag_ref,
     y_ref, s_out, last_rkv, last_lo,
     s_s, car_rkv, car_lo, nkk_s, w_s, b_s, k_s, v_s, r_s, yy_s) = refs[2 * n_src:]
    c = pl.program_id(1)

    @pl.when(c == 0)
    def _():
        for j in range(nch):
            for h in range(H_R):
                s_s[j, :, h * N_R:(h + 1) * N_R] = s0_ref[j, h]
        car_rkv[...] = sh_rkv_ref[...]
        car_lo[...] = sh_lo_ref[...]

    def slab(src):
        return src[0][...] if n_src == 1 else jnp.concatenate([r[...] for r in src], axis=0)

    def shift(x, car_ref, mu):
        rolled = pltpu.roll(x, shift=1, axis=0)
        car = jnp.concatenate(
            [jnp.broadcast_to(car_ref[j:j + 1, :], (L, x.shape[1])) for j in range(nch)], axis=0)
        prev = jnp.where(_iota2(x.shape, 0) % L == 0, car, rolled)
        return x + (prev - x) * mu

    rkv = slab(rkv_refs)
    lo = slab(lo_refs)
    xs = shift(rkv, car_rkv, mu_rkv_ref[...])
    xl = shift(lo, car_lo, mu_lo_ref[...])
    for j in range(nch):
        car_rkv[j:j + 1, :] = rkv[(j + 1) * L - 1:(j + 1) * L, :]
        car_lo[j:j + 1, :] = lo[(j + 1) * L - 1:(j + 1) * L, :]

    r = xs[:, 0:D_R]
    k = xs[:, D_R:2 * D_R]
    v = xs[:, 2 * D_R:3 * D_R]
    w = -_softplus(-(w0_ref[...] + _dot(jnp.tanh(xl), wup_ref[...]))) - 0.5
    decay = jnp.exp(-jnp.exp(w))
    a = _sigmoid(a0_ref[...] + _dot(xl, aup_ref[...]))
    g = _dot(_sigmoid(xl), gup_ref[...])
    kk = k * kk_ref[...]
    kk = kk / jnp.maximum(jnp.sqrt(_seg_sum(kk * kk, e_ref, 2)), 1e-12)
    k = k * (1.0 + (a - 1.0) * ka_ref[...])

    nkk_s[...] = -kk
    w_s[...] = decay
    b_s[...] = kk * a
    k_s[...] = k
    v_s[...] = v
    r_s[...] = r

    diag = diag_ref[...]
    diag_b = diag.astype(BF16)
    e = e_ref[...]

    def head_sums(lhs):
        return jnp.concatenate(
            [jnp.dot(lhs[:, i * 256:(i + 1) * 256], e, preferred_element_type=F32)
             for i in range(D_R // 256)], axis=1)

    def put_y(j, t, yexp):
        yy_s[pl.ds(8 + j * L + t, 1), :] = jnp.sum(yexp * diag, axis=0, keepdims=True)

    def step(t, carry):
        tp = jnp.maximum(t - 1, 0)
        parts = []
        for j in range(nch):
            rowb = lambda ref, tt: ref[pl.ds(j * L + tt, 1), :].astype(BF16)
            sb = s_s[j].astype(BF16)
            parts += [sb * rowb(nkk_s, t), diag_b * rowb(v_s, t), sb * rowb(r_s, tp)]
        res = head_sums(jnp.concatenate(parts, axis=0))
        for j in range(nch):
            row = lambda ref: ref[pl.ds(j * L + t, 1), :]
            sa, vexp, yexp = (res[(3 * j + i) * N_R:(3 * j + i + 1) * N_R] for i in range(3))
            s_s[j] = s_s[j] * row(w_s) + sa * row(b_s) + vexp * row(k_s)
            put_y(j, t - 1, yexp)
        return carry

    lax.fori_loop(0, L, step, 0, unroll=4)
    last = jnp.concatenate(
        [s_s[j].astype(BF16) * r_s[(j + 1) * L - 1:(j + 1) * L, :].astype(BF16) for j in range(nch)], axis=0)
    res = head_sums(last)
    for j in range(nch):
        put_y(j, L - 1, res[j * N_R:(j + 1) * N_R])

    y = yy_s[8:, :]
    mean = _seg_sum(y, e_ref, 2) * (1.0 / N_R)
    d = y - mean
    var = _seg_sum(d * d, e_ref, 2) * (1.0 / N_R)
    y = d * lax.rsqrt(var + GN_EPS) * lnw_ref[...] + lnb_ref[...]
    y = (y + _seg_sum(r * k * rk_ref[...], e_ref, 2) * v) * g
    for j in range(nch):
        y_ref[j] = y[j * L:(j + 1) * L, :]

    @pl.when(c == nc - 1)
    def _():
        for j in range(nch):
            for h in range(H_R):
                s_out[j, h] = s_s[j, :, h * N_R:(h + 1) * N_R]
            last_rkv[j:j + 1, :] = rkv[(j + 1) * L - 1:(j + 1) * L, :]
            last_lo[j:j + 1, :] = lo[(j + 1) * L - 1:(j + 1) * L, :]


def _rwkv(proj, row0, B, T, nch, s_all, sl, shift0, p, layer):
    L = math.gcd(T, CHUNK)
    nc = T // L
    assert B % nch == 0
    sh_r, sh_xw, sh_k, sh_v, sh_xa, sh_xg = jnp.split(
        shift0, [D_R, D_R + W_LORA, 2 * D_R + W_LORA, 3 * D_R + W_LORA, 3 * D_R + W_LORA + A_LORA], axis=-1)
    sh_rkv = jnp.concatenate([sh_r, sh_k, sh_v], axis=-1)
    sh_lo = jnp.concatenate(
        [sh_xw, sh_xa, sh_xg, jnp.zeros((B, LORA_W - W_LORA - A_LORA - G_LORA), F32)], axis=-1)
    lane = jnp.arange(256)
    e256 = (lane[:, None] // N_R == lane[None, :] // N_R).astype(BF16)
    diag = (jnp.arange(N_R)[:, None] == (jnp.arange(D_R)[None, :] % N_R)).astype(F32)
    per_layer = lambda a: pl.BlockSpec((None,) + a.shape[1:], lambda g, c: (layer,) + (0,) * (a.ndim - 1))
    const = lambda a: pl.BlockSpec(a.shape, lambda g, c: (0,) * a.ndim)
    params = [p["mu_rkv"], p["mu_lo"], p["w0"], p["a0"], p["w_up"], p["a_up"], p["g_up"],
              p["k_k"], p["k_a"], p["r_k"], p["ln_w"], p["ln_b"]]
    if nc == 1:
        rows = nch * L
        assert row0 % rows == 0
        n_src = 1
        src = lambda w, col: [pl.BlockSpec((rows, w), lambda g, c: (row0 // rows + g, col // w))]
    else:
        n_src = nch
        src = lambda w, col: [
            pl.BlockSpec((L, w), lambda g, c, j=j: (row0 // L + (g * nch + j) * nc + c, col // w))
            for j in range(nch)]
    y, s_new, last_rkv, last_lo = pl.pallas_call(
        functools.partial(_rwkv_body, L, nc, nch, n_src),
        grid=(B // nch, nc),
        in_specs=src(3 * D_R, C_RKV) + src(LORA_W, C_LORA) + [
            pl.BlockSpec((None, nch, H_R, N_R, N_R), lambda g, c: (sl, g, 0, 0, 0)),
            pl.BlockSpec((nch, 3 * D_R), lambda g, c: (g, 0)),
            pl.BlockSpec((nch, LORA_W), lambda g, c: (g, 0)),
        ] + [per_layer(a) for a in params] + [const(e256), const(diag)],
        out_specs=[
            pl.BlockSpec((nch, L, D_R), lambda g, c: (g, c, 0)),
            pl.BlockSpec((nch, H_R, N_R, N_R), lambda g, c: (g, 0, 0, 0)),
            pl.BlockSpec((nch, 3 * D_R), lambda g, c: (g, 0)),
            pl.BlockSpec((nch, LORA_W), lambda g, c: (g, 0)),
        ],
        out_shape=[
            jax.ShapeDtypeStruct((B, T, D_R), F32),
            jax.ShapeDtypeStruct((B, H_R, N_R, N_R), F32),
            jax.ShapeDtypeStruct((B, 3 * D_R), F32),
            jax.ShapeDtypeStruct((B, LORA_W), F32),
        ],
        scratch_shapes=[pltpu.VMEM((nch, N_R, D_R), F32), pltpu.VMEM((nch, 3 * D_R), F32),
                        pltpu.VMEM((nch, LORA_W), F32)]
        + [pltpu.VMEM((nch * L, D_R), F32)] * 6 + [pltpu.VMEM((nch * L + 8, D_R), F32)],
        compiler_params=pltpu.CompilerParams(
            dimension_semantics=("arbitrary", "arbitrary"), vmem_limit_bytes=VMEM_LIMIT),
        name="rwkv",
    )(*([proj] * (2 * n_src)), s_all, sh_rkv, sh_lo, *params, e256, diag)
    shift_new = jnp.concatenate(
        [last_rkv[:, 0:D_R], last_lo[:, 0:W_LORA], last_rkv[:, D_R:3 * D_R],
         last_lo[:, W_LORA:W_LORA + A_LORA + G_LORA]], axis=-1)
    return y.reshape(B * T, D_R), s_new, shift_new


def _gla_body(L, nc, nb, q_ref, k_ref, v_ref, gg_ref, xa_ref, s0_ref, aup_ref, ab_ref, ng_ref,
              y_ref, s_out, s_s, bc_s):
    c = pl.program_id(1)

    @pl.when(c == 0)
    def _():
        s_s[...] = s0_ref[...]

    la = _log_sigmoid(_dot(xa_ref[...], aup_ref[...]) + ab_ref[...]) * (1.0 / GLA_TAU)
    tril = _tril_ones(L)
    for i in range(nb):
        bc_s[i * L:(i + 1) * L, :] = _dot_hi(tril, la[i * L:(i + 1) * L, :])
    ls = min(L, 16)
    eye = (_iota2((DK_G, DK_G), 0) == _iota2((DK_G, DK_G), 1)).astype(F32)

    units = [(i, h) for i in range(nb) for h in range(H_G)]
    us = range(len(units))
    rows = [slice(i * L, (i + 1) * L) for i, _ in units]
    ksl = [slice(h * DK_G, (h + 1) * DK_G) for _, h in units]
    vsl = [slice(h * DV_G, (h + 1) * DV_G) for _, h in units]
    s_old = [s_s[i, h] for i, h in units]
    q = [q_ref[rows[u], ksl[u]] * (DK_G ** -0.5) for u in us]
    k = [k_ref[rows[u], ksl[u]] for u in us]
    bc = [bc_s[rows[u], ksl[u]] for u in us]
    b_end = [bc_s[units[u][0] * L + L - 1:units[u][0] * L + L, ksl[u]] for u in us]
    inter = [_dot(q[u] * jnp.exp(bc[u]), s_old[u]) for u in us]
    kv = [_dot_tn(k[u] * jnp.exp(b_end[u] - bc[u]), v_ref[rows[u], vsl[u]]) for u in us]
    sdec = [_dot_hi(eye * jnp.exp(b_end[u]), s_old[u]) for u in us]
    nsub = L // ls
    att = [[None] * nsub for _ in us]
    for j in range(nsub):
        lo, hi = j * ls, (j + 1) * ls
        for u in us:
            r0 = units[u][0] * L
            rho = bc_s[r0 + lo - 1:r0 + lo, ksl[u]] if j > 0 else jnp.zeros((1, DK_G), F32)
            qi = q[u][lo:hi] * jnp.exp(bc[u][lo:hi] - rho)
            ki = k[u][0:hi] * jnp.exp(rho - bc[u][0:hi])
            a = _dot_nt(qi, ki)
            att[u][j] = jnp.where(_iota2(a.shape, 1) <= _iota2(a.shape, 0) + lo, a, 0.0)
    intra = [[_dot(att[u][j], v_ref[units[u][0] * L:units[u][0] * L + (j + 1) * ls, vsl[u]])
              for j in range(nsub)] for u in us]
    s_new, y_new = [], []
    for u in us:
        o = inter[u] + jnp.concatenate(intra[u], axis=0)
        s_new.append(sdec[u] + kv[u])
        gg = gg_ref[rows[u], vsl[u]]
        y_new.append(_rms(o, ng_ref[:, vsl[u]]) * (gg * _sigmoid(gg)))

    for u, (i, h) in enumerate(units):
        s_s[i, h] = s_new[u]
        y_ref[rows[u], vsl[u]] = y_new[u]

    @pl.when(c == nc - 1)
    def _():
        s_out[...] = s_s[...]


def _gla(proj, row0, B, T, nb, s_all, sl, alpha_up, alpha_b, norm_g, layer):
    L = math.gcd(T, CHUNK)
    nc = T // L
    R = nb * L
    assert (nb == 1 or nc == 1) and B % nb == 0 and row0 % R == 0
    rb = row0 // R
    row = lambda w, j: pl.BlockSpec((R, w), lambda g, c: (rb + g * nc + c, j))
    y, s_new = pl.pallas_call(
        functools.partial(_gla_body, L, nc, nb),
        grid=(B // nb, nc),
        in_specs=[
            row(K_G, C_QG // K_G), row(K_G, C_KG // K_G), row(D_G, C_VG // D_G), row(D_G, C_GG // D_G),
            row(XA_W, C_XA // XA_W),
            pl.BlockSpec((None, nb, H_G, DK_G, DV_G), lambda g, c: (sl, g, 0, 0, 0)),
            pl.BlockSpec((None, XA_W, K_G), lambda g, c: (layer, 0, 0)),
            pl.BlockSpec((None, 1, K_G), lambda g, c: (layer, 0, 0)),
            pl.BlockSpec((None, 1, D_G), lambda g, c: (layer, 0, 0)),
        ],
        out_specs=[
            pl.BlockSpec((R, D_G), lambda g, c: (g * nc + c, 0)),
            pl.BlockSpec((nb, H_G, DK_G, DV_G), lambda g, c: (g, 0, 0, 0)),
        ],
        out_shape=[
            jax.ShapeDtypeStruct((B * T, D_G), F32),
            jax.ShapeDtypeStruct((B, H_G, DK_G, DV_G), F32),
        ],
        scratch_shapes=[pltpu.VMEM((nb, H_G, DK_G, DV_G), F32), pltpu.VMEM((R, K_G), F32)],
        compiler_params=pltpu.CompilerParams(
            dimension_semantics=("arbitrary", "arbitrary"), vmem_limit_bytes=VMEM_LIMIT),
        name="gla",
    )(proj, proj, proj, proj, proj, s_all, alpha_up, alpha_b, norm_g)
    return y, s_new


def _permute_cols(w_in):
    pm, pr, pg = jnp.split(w_in, [P_M, P_M + P_R], axis=-1)
    qm, km, vm, om, gates = jnp.split(pm, [512, 1024, 2048, 3072], axis=-1)
    r, xw, k, v, xa, xg = jnp.split(
        pr, [D_R, D_R + W_LORA, 2 * D_R + W_LORA, 3 * D_R + W_LORA, 3 * D_R + W_LORA + A_LORA], axis=-1)
    qg, kg, vg, xag, gg = jnp.split(pg, [K_G, 2 * K_G, 2 * K_G + D_G, 2 * K_G + D_G + ALPHA_LORA], axis=-1)
    zeros = lambda n: jnp.zeros(w_in.shape[:-1] + (n,), w_in.dtype)
    cols = [qm, km, vm, om, r, k, v, gg, vg, qg, kg,
            xw, xa, xg, zeros(LORA_W - W_LORA - A_LORA - G_LORA),
            gates, zeros(GATE_W - 2 * H_M), xag, zeros(XA_W - ALPHA_LORA)]
    return jnp.concatenate(cols, axis=-1)


def _rows_at(w, start, total):
    n = w.shape[1]
    return jnp.pad(w, ((0, 0), (start, total - start - n), (0, 0))).astype(BF16)


def _prep_rwkv(mu, w0, w_up, a0, a_up, g_up, k_k, k_a, r_k, ln_w, ln_b):
    depth = mu.shape[0]
    mu_r, mu_xw, mu_k, mu_v, mu_xa, mu_xg = jnp.split(
        mu, [D_R, D_R + W_LORA, 2 * D_R + W_LORA, 3 * D_R + W_LORA, 3 * D_R + W_LORA + A_LORA], axis=-1)
    row = lambda a: a.reshape(depth, 1, -1).astype(F32)
    return dict(
        mu_rkv=row(jnp.concatenate([mu_r, mu_k, mu_v], axis=-1)),
        mu_lo=row(jnp.concatenate(
            [mu_xw, mu_xa, mu_xg, jnp.zeros((depth, LORA_W - W_LORA - A_LORA - G_LORA), mu.dtype)], axis=-1)),
        w0=row(w0), a0=row(a0),
        w_up=_rows_at(w_up, 0, LORA_W),
        a_up=_rows_at(a_up, W_LORA, LORA_W),
        g_up=_rows_at(g_up, W_LORA + A_LORA, LORA_W),
        k_k=row(k_k), k_a=row(k_a), r_k=row(r_k), ln_w=row(ln_w), ln_b=row(ln_b),
    )


def _pad_m(m):
    return jnp.pad(m.astype(F32), ((0, 0), (0, 0), (0, GATE_W - H_M)))[:, :, None, :]


def kernel(x_prompt, x_sample, state_mlstm_c, state_mlstm_n, state_mlstm_m, state_rwkv_s, state_rwkv_shift, state_gla_s, ffn1_pre_g, ffn1_post_g, ffn1_w_gate, ffn1_w_up, ffn1_w_down, mix_pre_g, mix_post_g, w_in, w_out, mlstm_b_i, mlstm_b_f, mlstm_norm_g, rwkv_mu, rwkv_w0, rwkv_w_up, rwkv_a0, rwkv_a_up, rwkv_g_up, rwkv_k_k, rwkv_k_a, rwkv_r_k, rwkv_ln_w, rwkv_ln_b, gla_alpha_up, gla_alpha_b, gla_norm_g, ffn2_pre_g, ffn2_post_g, ffn2_w_gate, ffn2_w_up, ffn2_w_down):
    depth = w_in.shape[0]
    bp, tp, d = x_prompt.shape
    bs, ts, _ = x_sample.shape
    mp, ms = bp * tp, bs * ts
    m = mp + ms
    tm_ffn, tm_proj, tm_out = 512, 1024, 512
    assert m % tm_ffn == 0 and m % tm_proj == 0 and mp % tm_out == 0 and ms % tm_out == 0

    row = lambda a: a.reshape(depth, 1, -1).astype(F32)
    bf = lambda a: a.astype(BF16)
    f1 = (row(ffn1_pre_g), row(ffn1_post_g), bf(ffn1_w_gate), bf(ffn1_w_up), bf(ffn1_w_down))
    f2 = (row(ffn2_pre_g), row(ffn2_post_g), bf(ffn2_w_gate), bf(ffn2_w_up), bf(ffn2_w_down))
    w_in_p = _permute_cols(w_in).astype(BF16)
    w_out_b = bf(w_out)
    mix_pre, mix_post = row(mix_pre_g), row(mix_post_g)
    m_bias = row(jnp.concatenate(
        [mlstm_b_i, mlstm_b_f, jnp.zeros((depth, GATE_W - 2 * H_M), mlstm_b_i.dtype)], axis=-1))
    m_norm = row(mlstm_norm_g)
    rw = _prep_rwkv(rwkv_mu, rwkv_w0, rwkv_w_up, rwkv_a0, rwkv_a_up, rwkv_g_up, rwkv_k_k, rwkv_k_a,
                    rwkv_r_k, rwkv_ln_w, rwkv_ln_b)
    g_aup = _rows_at(gla_alpha_up, 0, XA_W)
    g_ab, g_norm = row(gla_alpha_b), row(gla_norm_g)

    zp = lambda *s: jnp.zeros((1, bp) + s, F32)
    st_p = (zp(H_M, DK_M, DV_M), zp(H_M, DK_M), _pad_m(zp(H_M)), zp(H_R, N_R, N_R), zp(P_R), zp(H_G, DK_G, DV_G))
    st_s = (state_mlstm_c.astype(F32), state_mlstm_n.astype(F32), _pad_m(state_mlstm_m),
            state_rwkv_s.astype(F32), state_rwkv_shift.astype(F32), state_gla_s.astype(F32))

    x = jnp.concatenate([x_prompt.reshape(mp, d), x_sample.reshape(ms, d)], axis=0)
    new_p = [[] for _ in range(6)]
    new_s = [[] for _ in range(6)]
    for l in range(depth):
        x = _ffn(x, *f1, l, tm_ffn, 512)
        proj = _inproj(x, mix_pre, w_in_p, l, tm_proj, 512)
        ys = []
        for (row0, B, T, nch, nb, st, sl, acc) in ((0, bp, tp, bp, 1, st_p, 0, new_p), (mp, bs, ts, 8, 4, st_s, l, new_s)):
            ym, C, n, mm = _mlstm(proj, row0, B, T, nb, st[0], st[1], st[2], sl, m_bias, m_norm, l)
            yr, Sr, sh = _rwkv(proj, row0, B, T, nch, st[3], sl, st[4][sl], rw, l)
            yg, Sg = _gla(proj, row0, B, T, nb, st[5], sl, g_aup, g_ab, g_norm, l)
            ys.append((ym, yr, yg))
            for j, a in enumerate((C, n, mm, Sr, sh, Sg)):
                acc[j].append(a)
        x = _outproj(x, ys[0], ys[1], w_out_b, mix_post, l, tm_out)
        x = _ffn(x, *f2, l, tm_ffn, 512)

    dts = (state_mlstm_c.dtype, state_mlstm_n.dtype, state_mlstm_m.dtype,
           state_rwkv_s.dtype, state_rwkv_shift.dtype, state_gla_s.dtype)
    outs_p = [jnp.stack(a).astype(dt) for a, dt in zip(new_p, dts)]
    outs_s = [jnp.stack(a).astype(dt) for a, dt in zip(new_s, dts)]
    return (x[:mp].reshape(bp, tp, d), x[mp:].reshape(bs, ts, d), *outs_p, *outs_s)
```

```python
import functools
import math

import jax
import jax.numpy as jnp
from jax import lax
from jax.experimental import pallas as pl
from jax.experimental.pallas import tpu as pltpu

F32 = jnp.float32
BF16 = jnp.bfloat16

D_MODEL = 2048
D_FF = 5632
NORM_EPS = 1e-6
GN_EPS = 64e-5
GLA_TAU = 16.0
CHUNK = 64

H_M, DK_M, DV_M = 8, 64, 128
D_M = H_M * DV_M
H_R, N_R = 8, 64
D_R = H_R * N_R
W_LORA, A_LORA, G_LORA = 32, 32, 96
P_R = 3 * D_R + W_LORA + A_LORA + G_LORA
H_G, DK_G, DV_G = 4, 64, 128
D_G = H_G * DV_G
K_G = H_G * DK_G
ALPHA_LORA = 16
P_M = 2 * H_M * DK_M + 2 * D_M + 2 * H_M
P_G = 2 * K_G + 2 * D_G + ALPHA_LORA

C_QM, C_KM, C_VM, C_OM = 0, 512, 1024, 2048
C_RKV = 3072
C_GG, C_VG, C_QG, C_KG = 4608, 5120, 5632, 5888
C_LORA, LORA_W = 6144, 256
C_GATE, GATE_W = 6400, 128
C_XA, XA_W = 6528, 128
P_PAD = 6656

VMEM_LIMIT = 56 * 1024 * 1024
HI = lax.Precision.HIGHEST


def _dot(a, b):
    return jnp.dot(a.astype(BF16), b.astype(BF16), preferred_element_type=F32)


def _dot_nt(a, b):
    return lax.dot_general(a.astype(BF16), b.astype(BF16), (((1,), (1,)), ((), ())),
                           preferred_element_type=F32)


def _dot_tn(a, b):
    return lax.dot_general(a.astype(BF16), b.astype(BF16), (((0,), (0,)), ((), ())),
                           preferred_element_type=F32)


def _dot_hi(a, b):
    return jnp.dot(a, b, precision=HI, preferred_element_type=F32)


def _dot_nt_hi(a, b):
    return lax.dot_general(a, b, (((1,), (1,)), ((), ())), precision=HI,
                           preferred_element_type=F32)


def _rms(x, g):
    return x * lax.rsqrt(jnp.mean(x * x, axis=-1, keepdims=True) + NORM_EPS) * g


def _log_sigmoid(x):
    return jnp.minimum(x, 0.0) - jnp.log(1.0 + jnp.exp(-jnp.abs(x)))


def _softplus(x):
    return jnp.maximum(x, 0.0) + jnp.log(1.0 + jnp.exp(-jnp.abs(x)))


def _sigmoid(x):
    return 1.0 / (1.0 + jnp.exp(-x))


def _iota2(shape, dim):
    return lax.broadcasted_iota(jnp.int32, shape, dim)


def _tril_ones(n):
    return (_iota2((n, n), 1) <= _iota2((n, n), 0)).astype(F32)


def _ffn_body(nf, n_p, n_in, n_out, *refs):
    x_refs = refs[:n_in]
    pre_ref, post_ref, wg_ref, wu_ref, wd_ref = refs[n_in:n_in + 5]
    o_refs = refs[n_in + 5:n_in + 5 + n_out]
    h_ref = refs[-1]
    f = pl.program_id(1)

    def run(x_ref, o_ref):
        @pl.when(f == 0)
        def _():
            h_ref[...] = _rms(x_ref[...], pre_ref[...]).astype(BF16)
            o_ref[...] = jnp.zeros_like(o_ref)

        h = h_ref[...]
        g = jnp.dot(h, wg_ref[...], preferred_element_type=F32)
        u = jnp.dot(h, wu_ref[...], preferred_element_type=F32)
        a = (g * _sigmoid(g) * u).astype(BF16)
        o_ref[...] += jnp.dot(a, wd_ref[...], preferred_element_type=F32)

        @pl.when(f == nf - 1)
        def _():
            o_ref[...] = x_ref[...] + 0.5 * _rms(o_ref[...], post_ref[...])

    if n_in == 1 and n_out == 1:
        run(x_refs[0], o_refs[0])
    else:
        @pl.when(pl.program_id(0) < n_p)
        def _():
            run(x_refs[0], o_refs[0])

        @pl.when(pl.program_id(0) >= n_p)
        def _():
            run(x_refs[-1], o_refs[-1])


def _ffn(xs, pre_g, post_g, wg, wu, wd, layer, tm, tf, split_rows=None):
    xs = xs if isinstance(xs, tuple) else (xs,)
    d = xs[0].shape[1]
    m = sum(x.shape[0] for x in xs)
    n_p = (xs[0].shape[0] if len(xs) == 2 else (split_rows or 0)) // tm
    assert all(x.shape[0] % tm == 0 for x in xs) and (split_rows or 0) % tm == 0
    nf = D_FF // tf
    whole = pl.BlockSpec((tm, d), lambda i, f: (i, 0))
    first = pl.BlockSpec((tm, d), lambda i, f: (jnp.minimum(i, n_p - 1), 0))
    rest = pl.BlockSpec((tm, d), lambda i, f: (jnp.maximum(i - n_p, 0), 0))
    out_shape = jax.ShapeDtypeStruct((m, d), F32)
    out_specs = whole
    if split_rows:
        out_shape = [jax.ShapeDtypeStruct((split_rows, d), F32), jax.ShapeDtypeStruct((m - split_rows, d), F32)]
        out_specs = [first, rest]
    return pl.pallas_call(
        functools.partial(_ffn_body, nf, n_p, len(xs), 2 if split_rows else 1),
        grid=(m // tm, nf),
        in_specs=([first, rest] if len(xs) == 2 else [whole]) + [
            pl.BlockSpec((None, 1, d), lambda i, f: (layer, 0, 0)),
            pl.BlockSpec((None, 1, d), lambda i, f: (layer, 0, 0)),
            pl.BlockSpec((None, d, tf), lambda i, f: (layer, 0, f)),
            pl.BlockSpec((None, d, tf), lambda i, f: (layer, 0, f)),
            pl.BlockSpec((None, tf, d), lambda i, f: (layer, f, 0)),
        ],
        out_specs=out_specs,
        out_shape=out_shape,
        scratch_shapes=[pltpu.VMEM((tm, d), BF16)],
        compiler_params=pltpu.CompilerParams(
            dimension_semantics=("arbitrary", "arbitrary"), vmem_limit_bytes=VMEM_LIMIT),
        name="ffn",
    )(*xs, pre_g, post_g, wg, wu, wd)


def _inproj_body(x_ref, g_ref, w_ref, o_ref, h_ref):
    @pl.when(pl.program_id(1) == 0)
    def _():
        h_ref[...] = _rms(x_ref[...], g_ref[...]).astype(BF16)

    o_ref[...] = jnp.dot(h_ref[...], w_ref[...], preferred_element_type=F32)


def _inproj(x, pre_g, w_in, layer, tm, tn):
    m, d = x.shape
    return pl.pallas_call(
        _inproj_body,
        grid=(m // tm, P_PAD // tn),
        in_specs=[
            pl.BlockSpec((tm, d), lambda i, j: (i, 0)),
            pl.BlockSpec((None, 1, d), lambda i, j: (layer, 0, 0)),
            pl.BlockSpec((None, d, tn), lambda i, j: (layer, 0, j)),
        ],
        out_specs=pl.BlockSpec((tm, tn), lambda i, j: (i, j)),
        out_shape=jax.ShapeDtypeStruct((m, P_PAD), F32),
        scratch_shapes=[pltpu.VMEM((tm, d), BF16)],
        compiler_params=pltpu.CompilerParams(
            dimension_semantics=("arbitrary", "arbitrary"), vmem_limit_bytes=VMEM_LIMIT),
        name="inproj",
    )(x, pre_g, w_in)


def _outproj_body(n_p, x_ref, ymp_ref, yrp_ref, ygp_ref, yms_ref, yrs_ref, ygs_ref, w_ref, g_ref, o_ref):
    def run(ym_ref, yr_ref, yg_ref):
        y = jnp.dot(ym_ref[...].astype(BF16), w_ref[0:D_M, :], preferred_element_type=F32)
        y += jnp.dot(yr_ref[...].astype(BF16), w_ref[D_M:D_M + D_R, :], preferred_element_type=F32)
        y += jnp.dot(yg_ref[...].astype(BF16), w_ref[D_M + D_R:, :], preferred_element_type=F32)
        o_ref[...] = x_ref[...] + _rms(y, g_ref[...])

    @pl.when(pl.program_id(0) < n_p)
    def _():
        run(ymp_ref, yrp_ref, ygp_ref)

    @pl.when(pl.program_id(0) >= n_p)
    def _():
        run(yms_ref, yrs_ref, ygs_ref)


def _outproj(x, y_p, y_s, w_out, post_g, layer, tm):
    m, d = x.shape
    n_p = y_p[0].shape[0] // tm
    assert y_p[0].shape[0] % tm == 0 and y_s[0].shape[0] % tm == 0
    pspec = lambda w: pl.BlockSpec((tm, w), lambda i: (jnp.minimum(i, n_p - 1), 0))
    sspec = lambda w: pl.BlockSpec((tm, w), lambda i: (jnp.maximum(i - n_p, 0), 0))
    return pl.pallas_call(
        functools.partial(_outproj_body, n_p),
        grid=(m // tm,),
        in_specs=[
            pl.BlockSpec((tm, d), lambda i: (i, 0)),
            pspec(D_M), pspec(D_R), pspec(D_G), sspec(D_M), sspec(D_R), sspec(D_G),
            pl.BlockSpec((None, d, d), lambda i: (layer, 0, 0)),
            pl.BlockSpec((None, 1, d), lambda i: (layer, 0, 0)),
        ],
        out_specs=pl.BlockSpec((tm, d), lambda i: (i, 0)),
        out_shape=jax.ShapeDtypeStruct((m, d), F32),
        compiler_params=pltpu.CompilerParams(
            dimension_semantics=("arbitrary",), vmem_limit_bytes=VMEM_LIMIT),
        name="outproj",
    )(x, *y_p, *y_s, w_out, post_g)


def _mlstm_body(L, nc, nb, q_ref, k_ref, v_ref, o_ref, gt_ref, c0_ref, n0_ref, m0_ref, bias_ref, ng_ref,
                y_ref, c_out, n_out, m_out, c_s, n_s, m_s):
    c = pl.program_id(1)

    @pl.when(c == 0)
    def _():
        c_s[...] = c0_ref[...]
        n_s[...] = n0_ref[...]
        m_s[...] = m0_ref[:, 0, :]

    eye = (_iota2((GATE_W, GATE_W), 0) == _iota2((GATE_W, GATE_W), 1)).astype(F32)
    causal = _iota2((L, L), 1) <= _iota2((L, L), 0)
    tril = _tril_ones(L)
    seqs = range(nb)
    rows = [slice(i * L, (i + 1) * L) for i in seqs]

    gates, gates_t = [], []
    for i in seqs:
        z = gt_ref[rows[i], :] + bias_ref[...]
        lane = _iota2(z.shape, 1)
        is_f = (lane >= H_M) & (lane < 2 * H_M)
        lf = jnp.where(is_f, _log_sigmoid(z), 0.0)
        gates.append(jnp.where(lane < H_M, z, _dot_hi(tril, lf)))
        gates_t.append(_dot_nt_hi(eye, gates[i]))

    units = [(i, h) for i in seqs for h in range(H_M)]
    c_old = [c_s[i, h] for i, h in units]
    n_old = [n_s[i] for i in seqs]
    m_old = [m_s[i:i + 1, :] for i in seqs]
    q = [q_ref[rows[i], h * DK_M:(h + 1) * DK_M] for i, h in units]
    k = [k_ref[rows[i], h * DK_M:(h + 1) * DK_M] * (DK_M ** -0.5) for i, h in units]
    v = [v_ref[rows[i], h * DV_M:(h + 1) * DV_M] for i, h in units]
    qk = [_dot_nt(q[u], k[u]) for u in range(len(units))]
    qc = [_dot(q[u], c_old[u]) for u in range(len(units))]

    dexp, w_inter, m_t, kw, sc = [], [], [], [], []
    m_new_all = [jnp.zeros_like(m_old[i]) for i in seqs]
    for u, (i, h) in enumerate(units):
        li_c = gates[i][:, h:h + 1]
        b_c = gates[i][:, H_M + h:H_M + h + 1]
        li_r = gates_t[i][h:h + 1, :]
        b_r = gates_t[i][H_M + h:H_M + h + 1, :]
        m_prev = m_old[i][:, h:h + 1]
        dmat = jnp.where(causal, b_c - b_r + li_r, -jnp.inf)
        inter = b_c + m_prev
        m_t.append(jnp.maximum(inter, jnp.max(dmat, axis=-1, keepdims=True)))
        dexp.append(jnp.exp(dmat - m_t[u]))
        w_inter.append(jnp.exp(inter - m_t[u]))
        g_end = b_c[L - 1:L, :]
        dec = g_end - b_c + li_c
        m_new = jnp.maximum(g_end + m_prev, jnp.max(dec, axis=0, keepdims=True))
        kw.append(k[u] * jnp.exp(dec - m_new))
        sc.append(jnp.exp(g_end + m_prev - m_new))
        m_new_all[i] = jnp.where(_iota2(m_old[i].shape, 1) == h, m_new, m_new_all[i])

    us = range(len(units))
    kv = [_dot_tn(kw[u], v[u]) for u in us]
    w_intra = [dexp[u] * qk[u] for u in us]
    ones_v = jnp.ones((L, DV_M), F32)
    wv = [_dot(w_intra[u], jnp.concatenate([v[u], ones_v], axis=1)) for u in us]
    n_rows = [n_old[i][h:h + 1, :] for i, h in units]
    qn = [_dot_nt(q[u], jnp.broadcast_to(n_rows[u], (DV_M, DK_M))) for u in us]
    hh = []
    for u in us:
        num = w_inter[u] * qc[u] + wv[u][:, :DV_M]
        den = w_inter[u] * qn[u] + wv[u][:, DV_M:]
        hh.append(num / jnp.maximum(jnp.abs(den), jnp.exp(-m_t[u])))
    ones_m = jnp.ones((DV_M, DV_M), F32)
    ms = [_dot(hh[u] * hh[u], ones_m) * (1.0 / DV_M) for u in us]

    c_new, y_new = [], []
    n_new = [jnp.zeros_like(n_old[i]) for i in seqs]
    for u, (i, h) in enumerate(units):
        c_new.append(sc[u] * c_old[u] + kv[u])
        n_new[i] = jnp.where(_iota2(n_old[i].shape, 0) == h,
                             sc[u] * n_rows[u] + jnp.sum(kw[u], axis=0, keepdims=True), n_new[i])
        sl = slice(h * DV_M, (h + 1) * DV_M)
        y_new.append(hh[u] * lax.rsqrt(ms[u] + NORM_EPS) * ng_ref[:, sl] * _sigmoid(o_ref[rows[i], sl]))

    for u, (i, h) in enumerate(units):
        c_s[i, h] = c_new[u]
        y_ref[rows[i], h * DV_M:(h + 1) * DV_M] = y_new[u]
    for i in seqs:
        n_s[i] = n_new[i]
        m_s[i:i + 1, :] = m_new_all[i]

    @pl.when(c == nc - 1)
    def _():
        c_out[...] = c_s[...]
        n_out[...] = n_s[...]
        m_out[:, 0, :] = m_s[...]


def _mlstm(proj, row0, B, T, nb, c_all, n_all, m_all, sl, bias, norm_g, layer):
    L = math.gcd(T, CHUNK)
    nc = T // L
    R = nb * L
    assert (nb == 1 or nc == 1) and B % nb == 0 and row0 % R == 0
    rb = row0 // R
    row = lambda w, j: pl.BlockSpec((R, w), lambda g, c: (rb + g * nc + c, j))
    y, c_new, n_new, m_new = pl.pallas_call(
        functools.partial(_mlstm_body, L, nc, nb),
        grid=(B // nb, nc),
        in_specs=[
            row(512, C_QM // 512), row(512, C_KM // 512), row(D_M, C_VM // D_M), row(D_M, C_OM // D_M),
            row(GATE_W, C_GATE // GATE_W),
            pl.BlockSpec((None, nb, H_M, DK_M, DV_M), lambda g, c: (sl, g, 0, 0, 0)),
            pl.BlockSpec((None, nb, H_M, DK_M), lambda g, c: (sl, g, 0, 0)),
            pl.BlockSpec((None, nb, 1, GATE_W), lambda g, c: (sl, g, 0, 0)),
            pl.BlockSpec((None, 1, GATE_W), lambda g, c: (layer, 0, 0)),
            pl.BlockSpec((None, 1, D_M), lambda g, c: (layer, 0, 0)),
        ],
        out_specs=[
            pl.BlockSpec((R, D_M), lambda g, c: (g * nc + c, 0)),
            pl.BlockSpec((nb, H_M, DK_M, DV_M), lambda g, c: (g, 0, 0, 0)),
            pl.BlockSpec((nb, H_M, DK_M), lambda g, c: (g, 0, 0)),
            pl.BlockSpec((nb, 1, GATE_W), lambda g, c: (g, 0, 0)),
        ],
        out_shape=[
            jax.ShapeDtypeStruct((B * T, D_M), F32),
            jax.ShapeDtypeStruct((B, H_M, DK_M, DV_M), F32),
            jax.ShapeDtypeStruct((B, H_M, DK_M), F32),
            jax.ShapeDtypeStruct((B, 1, GATE_W), F32),
        ],
        scratch_shapes=[pltpu.VMEM((nb, H_M, DK_M, DV_M), F32), pltpu.VMEM((nb, H_M, DK_M), F32),
                        pltpu.VMEM((nb, GATE_W), F32)],
        compiler_params=pltpu.CompilerParams(
            dimension_semantics=("arbitrary", "arbitrary"), vmem_limit_bytes=VMEM_LIMIT),
        name="mlstm",
    )(proj, proj, proj, proj, proj, c_all, n_all, m_all, bias, norm_g)
    return y, c_new, n_new, m_new[:, 0, :H_M]


def _seg_sum(x, e_ref, passes):
    xh = x.astype(BF16)
    parts = [xh]
    if passes == 2:
        parts.append((x - xh.astype(F32)).astype(BF16))
    e = e_ref[...]
    halves = []
    for j in range(x.shape[1] // 256):
        sl = slice(j * 256, (j + 1) * 256)
        halves.append(sum(jnp.dot(p[:, sl], e, preferred_element_type=F32) for p in parts))
    return jnp.concatenate(halves, axis=1)


def _rwkv_body(L, nc, nch, n_src, *refs):
    rkv_refs, lo_refs = refs[:n_src], refs[n_src:2 * n_src]
    (s0_ref, sh_rkv_ref, sh_lo_ref, mu_rkv_ref, mu_lo_ref, w0_ref, a0_ref, wup_ref, aup_ref, gup_ref,
     kk_ref, ka_ref, rk_ref, lnw_ref, lnb_ref, e_ref, diag_ref,
     y_ref, s_out, last_rkv, last_lo,
     s_s, car_rkv, car_lo, nkk_s, w_s, b_s, k_s, v_s, r_s, yy_s) = refs[2 * n_src:]
    c = pl.program_id(1)

    @pl.when(c == 0)
    def _():
        for j in range(nch):
            for h in range(H_R):
                s_s[j, :, h * N_R:(h + 1) * N_R] = s0_ref[j, h]
        car_rkv[...] = sh_rkv_ref[...]
        car_lo[...] = sh_lo_ref[...]

    def slab(src):
        return src[0][...] if n_src == 1 else jnp.concatenate([r[...] for r in src], axis=0)

    def shift(x, car_ref, mu):
        rolled = pltpu.roll(x, shift=1, axis=0)
        car = jnp.concatenate(
            [jnp.broadcast_to(car_ref[j:j + 1, :], (L, x.shape[1])) for j in range(nch)], axis=0)
        prev = jnp.where(_iota2(x.shape, 0) % L == 0, car, rolled)
        return x + (prev - x) * mu

    rkv = slab(rkv_refs)
    lo = slab(lo_refs)
    xs = shift(rkv, car_rkv, mu_rkv_ref[...])
    xl = shift(lo, car_lo, mu_lo_ref[...])
    for j in range(nch):
        car_rkv[j:j + 1, :] = rkv[(j + 1) * L - 1:(j + 1) * L, :]
        car_lo[j:j + 1, :] = lo[(j + 1) * L - 1:(j + 1) * L, :]

    r = xs[:, 0:D_R]
    k = xs[:, D_R:2 * D_R]
    v = xs[:, 2 * D_R:3 * D_R]
    w = -_softplus(-(w0_ref[...] + _dot(jnp.tanh(xl), wup_ref[...]))) - 0.5
    decay = jnp.exp(-jnp.exp(w))
    a = _sigmoid(a0_ref[...] + _dot(xl, aup_ref[...]))
    g = _dot(_sigmoid(xl), gup_ref[...])
    kk = k * kk_ref[...]
    kk = kk / jnp.maximum(jnp.sqrt(_seg_sum(kk * kk, e_ref, 2)), 1e-12)
    k = k * (1.0 + (a - 1.0) * ka_ref[...])

    nkk_s[...] = -kk
    w_s[...] = decay
    b_s[...] = kk * a
    k_s[...] = k
    v_s[...] = v
    r_s[...] = r

    diag = diag_ref[...]
    diag_b = diag.astype(BF16)
    e = e_ref[...]

    def head_sums(lhs):
        return jnp.concatenate(
            [jnp.dot(lhs[:, i * 256:(i + 1) * 256], e, preferred_element_type=F32)
             for i in range(D_R // 256)], axis=1)

    def put_y(j, t, yexp):
        yy_s[pl.ds(8 + j * L + t, 1), :] = jnp.sum(yexp * diag, axis=0, keepdims=True)

    def step(t, carry):
        tp = jnp.maximum(t - 1, 0)
        parts = []
        for j in range(nch):
            rowb = lambda ref, tt: ref[pl.ds(j * L + tt, 1), :].astype(BF16)
            sb = s_s[j].astype(BF16)
            parts += [sb * rowb(nkk_s, t), diag_b * rowb(v_s, t), sb * rowb(r_s, tp)]
        res = head_sums(jnp.concatenate(parts, axis=0))
        for j in range(nch):
            row = lambda ref: ref[pl.ds(j * L + t, 1), :]
            sa, vexp, yexp = (res[(3 * j + i) * N_R:(3 * j + i + 1) * N_R] for i in range(3))
            s_s[j] = s_s[j] * row(w_s) + sa * row(b_s) + vexp * row(k_s)
            put_y(j, t - 1, yexp)
        return carry

    lax.fori_loop(0, L, step, 0, unroll=4)
    last = jnp.concatenate(
        [s_s[j].astype(BF16) * r_s[(j + 1) * L - 1:(j + 1) * L, :].astype(BF16) for j in range(nch)], axis=0)
    res = head_sums(last)
    for j in range(nch):
        put_y(j, L - 1, res[j * N_R:(j + 1) * N_R])

    y = yy_s[8:, :]
    mean = _seg_sum(y, e_ref, 2) * (1.0 / N_R)
    d = y - mean
    var = _seg_sum(d * d, e_ref, 2) * (1.0 / N_R)
    y = d * lax.rsqrt(var + GN_EPS) * lnw_ref[...] + lnb_ref[...]
    y = (y + _seg_sum(r * k * rk_ref[...], e_ref, 2) * v) * g
    for j in range(nch):
        y_ref[j] = y[j * L:(j + 1) * L, :]

    @pl.when(c == nc - 1)
    def _():
        for j in range(nch):
            for h in range(H_R):
                s_out[j, h] = s_s[j, :, h * N_R:(h + 1) * N_R]
            last_rkv[j:j + 1, :] = rkv[(j + 1) * L - 1:(j + 1) * L, :]
            last_lo[j:j + 1, :] = lo[(j + 1) * L - 1:(j + 1) * L, :]


def _rwkv(proj, row0, B, T, nch, s_all, sl, shift0, p, layer):
    L = math.gcd(T, CHUNK)
    nc = T // L
    assert B % nch == 0
    sh_r, sh_xw, sh_k, sh_v, sh_xa, sh_xg = jnp.split(
        shift0, [D_R, D_R + W_LORA, 2 * D_R + W_LORA, 3 * D_R + W_LORA, 3 * D_R + W_LORA + A_LORA], axis=-1)
    sh_rkv = jnp.concatenate([sh_r, sh_k, sh_v], axis=-1)
    sh_lo = jnp.concatenate(
        [sh_xw, sh_xa, sh_xg, jnp.zeros((B, LORA_W - W_LORA - A_LORA - G_LORA), F32)], axis=-1)
    lane = jnp.arange(256)
    e256 = (lane[:, None] // N_R == lane[None, :] // N_R).astype(BF16)
    diag = (jnp.arange(N_R)[:, None] == (jnp.arange(D_R)[None, :] % N_R)).astype(F32)
    per_layer = lambda a: pl.BlockSpec((None,) + a.shape[1:], lambda g, c: (layer,) + (0,) * (a.ndim - 1))
    const = lambda a: pl.BlockSpec(a.shape, lambda g, c: (0,) * a.ndim)
    params = [p["mu_rkv"], p["mu_lo"], p["w0"], p["a0"], p["w_up"], p["a_up"], p["g_up"],
              p["k_k"], p["k_a"], p["r_k"], p["ln_w"], p["ln_b"]]
    if nc == 1:
        rows = nch * L
        assert row0 % rows == 0
        n_src = 1
        src = lambda w, col: [pl.BlockSpec((rows, w), lambda g, c: (row0 // rows + g, col // w))]
    else:
        n_src = nch
        src = lambda w, col: [
            pl.BlockSpec((L, w), lambda g, c, j=j: (row0 // L + (g * nch + j) * nc + c, col // w))
            for j in range(nch)]
    y, s_new, last_rkv, last_lo = pl.pallas_call(
        functools.partial(_rwkv_body, L, nc, nch, n_src),
        grid=(B // nch, nc),
        in_specs=src(3 * D_R, C_RKV) + src(LORA_W, C_LORA) + [
            pl.BlockSpec((None, nch, H_R, N_R, N_R), lambda g, c: (sl, g, 0, 0, 0)),
            pl.BlockSpec((nch, 3 * D_R), lambda g, c: (g, 0)),
            pl.BlockSpec((nch, LORA_W), lambda g, c: (g, 0)),
        ] + [per_layer(a) for a in params] + [const(e256), const(diag)],
        out_specs=[
            pl.BlockSpec((nch, L, D_R), lambda g, c: (g, c, 0)),
            pl.BlockSpec((nch, H_R, N_R, N_R), lambda g, c: (g, 0, 0, 0)),
            pl.BlockSpec((nch, 3 * D_R), lambda g, c: (g, 0)),
            pl.BlockSpec((nch, LORA_W), lambda g, c: (g, 0)),
        ],
        out_shape=[
            jax.ShapeDtypeStruct((B, T, D_R), F32),
            jax.ShapeDtypeStruct((B, H_R, N_R, N_R), F32),
            jax.ShapeDtypeStruct((B, 3 * D_R), F32),
            jax.ShapeDtypeStruct((B, LORA_W), F32),
        ],
        scratch_shapes=[pltpu.VMEM((nch, N_R, D_R), F32), pltpu.VMEM((nch, 3 * D_R), F32),
                        pltpu.VMEM((nch, LORA_W), F32)]
        + [pltpu.VMEM((nch * L, D_R), F32)] * 6 + [pltpu.VMEM((nch * L + 8, D_R), F32)],
        compiler_params=pltpu.CompilerParams(
            dimension_semantics=("arbitrary", "arbitrary"), vmem_limit_bytes=VMEM_LIMIT),
        name="rwkv",
    )(*([proj] * (2 * n_src)), s_all, sh_rkv, sh_lo, *params, e256, diag)
    shift_new = jnp.concatenate(
        [last_rkv[:, 0:D_R], last_lo[:, 0:W_LORA], last_rkv[:, D_R:3 * D_R],
         last_lo[:, W_LORA:W_LORA + A_LORA + G_LORA]], axis=-1)
    return y.reshape(B * T, D_R), s_new, shift_new


def _gla_body(L, nc, nb, q_ref, k_ref, v_ref, gg_ref, xa_ref, s0_ref, aup_ref, ab_ref, ng_ref,
              y_ref, s_out, s_s, bc_s):
    c = pl.program_id(1)

    @pl.when(c == 0)
    def _():
        s_s[...] = s0_ref[...]

    la = _log_sigmoid(_dot(xa_ref[...], aup_ref[...]) + ab_ref[...]) * (1.0 / GLA_TAU)
    tril = _tril_ones(L)
    for i in range(nb):
        bc_s[i * L:(i + 1) * L, :] = _dot_hi(tril, la[i * L:(i + 1) * L, :])
    ls = min(L, 16)
    eye = (_iota2((DK_G, DK_G), 0) == _iota2((DK_G, DK_G), 1)).astype(F32)

    units = [(i, h) for i in range(nb) for h in range(H_G)]
    us = range(len(units))
    rows = [slice(i * L, (i + 1) * L) for i, _ in units]
    ksl = [slice(h * DK_G, (h + 1) * DK_G) for _, h in units]
    vsl = [slice(h * DV_G, (h + 1) * DV_G) for _, h in units]
    s_old = [s_s[i, h] for i, h in units]
    q = [q_ref[rows[u], ksl[u]] * (DK_G ** -0.5) for u in us]
    k = [k_ref[rows[u], ksl[u]] for u in us]
    bc = [bc_s[rows[u], ksl[u]] for u in us]
    b_end = [bc_s[units[u][0] * L + L - 1:units[u][0] * L + L, ksl[u]] for u in us]
    inter = [_dot(q[u] * jnp.exp(bc[u]), s_old[u]) for u in us]
    kv = [_dot_tn(k[u] * jnp.exp(b_end[u] - bc[u]), v_ref[rows[u], vsl[u]]) for u in us]
    sdec = [_dot_hi(eye * jnp.exp(b_end[u]), s_old[u]) for u in us]
    nsub = L // ls
    att = [[None] * nsub for _ in us]
    for j in range(nsub):
        lo, hi = j * ls, (j + 1) * ls
        for u in us:
            r0 = units[u][0] * L
            rho = bc_s[r0 + lo - 1:r0 + lo, ksl[u]] if j > 0 else jnp.zeros((1, DK_G), F32)
            qi = q[u][lo:hi] * jnp.exp(bc[u][lo:hi] - rho)
            ki = k[u][0:hi] * jnp.exp(rho - bc[u][0:hi])
            a = _dot_nt(qi, ki)
            att[u][j] = jnp.where(_iota2(a.shape, 1) <= _iota2(a.shape, 0) + lo, a, 0.0)
    intra = [[_dot(att[u][j], v_ref[units[u][0] * L:units[u][0] * L + (j + 1) * ls, vsl[u]])
              for j in range(nsub)] for u in us]
    s_new, y_new = [], []
    for u in us:
        o = inter[u] + jnp.concatenate(intra[u], axis=0)
        s_new.append(sdec[u] + kv[u])
        gg = gg_ref[rows[u], vsl[u]]
        y_new.append(_rms(o, ng_ref[:, vsl[u]]) * (gg * _sigmoid(gg)))

    for u, (i, h) in enumerate(units):
        s_s[i, h] = s_new[u]
        y_ref[rows[u], vsl[u]] = y_new[u]

    @pl.when(c == nc - 1)
    def _():
        s_out[...] = s_s[...]


def _gla(proj, row0, B, T, nb, s_all, sl, alpha_up, alpha_b, norm_g, layer):
    L = math.gcd(T, CHUNK)
    nc = T // L
    R = nb * L
    assert (nb == 1 or nc == 1) and B % nb == 0 and row0 % R == 0
    rb = row0 // R
    row = lambda w, j: pl.BlockSpec((R, w), lambda g, c: (rb + g * nc + c, j))
    y, s_new = pl.pallas_call(
        functools.partial(_gla_body, L, nc, nb),
        grid=(B // nb, nc),
        in_specs=[
            row(K_G, C_QG // K_G), row(K_G, C_KG // K_G), row(D_G, C_VG // D_G), row(D_G, C_GG // D_G),
            row(XA_W, C_XA // XA_W),
            pl.BlockSpec((None, nb, H_G, DK_G, DV_G), lambda g, c: (sl, g, 0, 0, 0)),
            pl.BlockSpec((None, XA_W, K_G), lambda g, c: (layer, 0, 0)),
            pl.BlockSpec((None, 1, K_G), lambda g, c: (layer, 0, 0)),
            pl.BlockSpec((None, 1, D_G), lambda g, c: (layer, 0, 0)),
        ],
        out_specs=[
            pl.BlockSpec((R, D_G), lambda g, c: (g * nc + c, 0)),
            pl.BlockSpec((nb, H_G, DK_G, DV_G), lambda g, c: (g, 0, 0, 0)),
        ],
        out_shape=[
            jax.ShapeDtypeStruct((B * T, D_G), F32),
            jax.ShapeDtypeStruct((B, H_G, DK_G, DV_G), F32),
        ],
        scratch_shapes=[pltpu.VMEM((nb, H_G, DK_G, DV_G), F32), pltpu.VMEM((R, K_G), F32)],
        compiler_params=pltpu.CompilerParams(
            dimension_semantics=("arbitrary", "arbitrary"), vmem_limit_bytes=VMEM_LIMIT),
        name="gla",
    )(proj, proj, proj, proj, proj, s_all, alpha_up, alpha_b, norm_g)
    return y, s_new


def _permute_cols(w_in):
    pm, pr, pg = jnp.split(w_in, [P_M, P_M + P_R], axis=-1)
    qm, km, vm, om, gates = jnp.split(pm, [512, 1024, 2048, 3072], axis=-1)
    r, xw, k, v, xa, xg = jnp.split(
        pr, [D_R, D_R + W_LORA, 2 * D_R + W_LORA, 3 * D_R + W_LORA, 3 * D_R + W_LORA + A_LORA], axis=-1)
    qg, kg, vg, xag, gg = jnp.split(pg, [K_G, 2 * K_G, 2 * K_G + D_G, 2 * K_G + D_G + ALPHA_LORA], axis=-1)
    zeros = lambda n: jnp.zeros(w_in.shape[:-1] + (n,), w_in.dtype)
    cols = [qm, km, vm, om, r, k, v, gg, vg, qg, kg,
            xw, xa, xg, zeros(LORA_W - W_LORA - A_LORA - G_LORA),
            gates, zeros(GATE_W - 2 * H_M), xag, zeros(XA_W - ALPHA_LORA)]
    return jnp.concatenate(cols, axis=-1)


def _rows_at(w, start, total):
    n = w.shape[1]
    return jnp.pad(w, ((0, 0), (start, total - start - n), (0, 0))).astype(BF16)


def _prep_rwkv(mu, w0, w_up, a0, a_up, g_up, k_k, k_a, r_k, ln_w, ln_b):
    depth = mu.shape[0]
    mu_r, mu_xw, mu_k, mu_v, mu_xa, mu_xg = jnp.split(
        mu, [D_R, D_R + W_LORA, 2 * D_R + W_LORA, 3 * D_R + W_LORA, 3 * D_R + W_LORA + A_LORA], axis=-1)
    row = lambda a: a.reshape(depth, 1, -1).astype(F32)
    return dict(
        mu_rkv=row(jnp.concatenate([mu_r, mu_k, mu_v], axis=-1)),
        mu_lo=row(jnp.concatenate(
            [mu_xw, mu_xa, mu_xg, jnp.zeros((depth, LORA_W - W_LORA - A_LORA - G_LORA), mu.dtype)], axis=-1)),
        w0=row(w0), a0=row(a0),
        w_up=_rows_at(w_up, 0, LORA_W),
        a_up=_rows_at(a_up, W_LORA, LORA_W),
        g_up=_rows_at(g_up, W_LORA + A_LORA, LORA_W),
        k_k=row(k_k), k_a=row(k_a), r_k=row(r_k), ln_w=row(ln_w), ln_b=row(ln_b),
    )


def _pad_m(m):
    return jnp.pad(m.astype(F32), ((0, 0), (0, 0), (0, GATE_W - H_M)))[:, :, None, :]


def kernel(x_prompt, x_sample, state_mlstm_c, state_mlstm_n, state_mlstm_m, state_rwkv_s, state_rwkv_shift, state_gla_s, ffn1_pre_g, ffn1_post_g, ffn1_w_gate, ffn1_w_up, ffn1_w_down, mix_pre_g, mix_post_g, w_in, w_out, mlstm_b_i, mlstm_b_f, mlstm_norm_g, rwkv_mu, rwkv_w0, rwkv_w_up, rwkv_a0, rwkv_a_up, rwkv_g_up, rwkv_k_k, rwkv_k_a, rwkv_r_k, rwkv_ln_w, rwkv_ln_b, gla_alpha_up, gla_alpha_b, gla_norm_g, ffn2_pre_g, ffn2_post_g, ffn2_w_gate, ffn2_w_up, ffn2_w_down):
    depth = w_in.shape[0]
    bp, tp, d = x_prompt.shape
    bs, ts, _ = x_sample.shape
    mp, ms = bp * tp, bs * ts
    m = mp + ms
    tm_ffn, tm_proj, tm_out = 512, 1024, 512
    assert m % tm_ffn == 0 and m % tm_proj == 0 and mp % tm_out == 0 and ms % tm_out == 0

    row = lambda a: a.reshape(depth, 1, -1).astype(F32)
    bf = lambda a: a.astype(BF16)
    f1 = (row(ffn1_pre_g), row(ffn1_post_g), bf(ffn1_w_gate), bf(ffn1_w_up), bf(ffn1_w_down))
    f2 = (row(ffn2_pre_g), row(ffn2_post_g), bf(ffn2_w_gate), bf(ffn2_w_up), bf(ffn2_w_down))
    w_in_p = _permute_cols(w_in).astype(BF16)
    w_out_b = bf(w_out)
    mix_pre, mix_post = row(mix_pre_g), row(mix_post_g)
    m_bias = row(jnp.concatenate(
        [mlstm_b_i, mlstm_b_f, jnp.zeros((depth, GATE_W - 2 * H_M), mlstm_b_i.dtype)], axis=-1))
    m_norm = row(mlstm_norm_g)
    rw = _prep_rwkv(rwkv_mu, rwkv_w0, rwkv_w_up, rwkv_a0, rwkv_a_up, rwkv_g_up, rwkv_k_k, rwkv_k_a,
                    rwkv_r_k, rwkv_ln_w, rwkv_ln_b)
    g_aup = _rows_at(gla_alpha_up, 0, XA_W)
    g_ab, g_norm = row(gla_alpha_b), row(gla_norm_g)

    zp = lambda *s: jnp.zeros((1, bp) + s, F32)
    st_p = (zp(H_M, DK_M, DV_M), zp(H_M, DK_M), _pad_m(zp(H_M)), zp(H_R, N_R, N_R), zp(P_R), zp(H_G, DK_G, DV_G))
    st_s = (state_mlstm_c.astype(F32), state_mlstm_n.astype(F32), _pad_m(state_mlstm_m),
            state_rwkv_s.astype(F32), state_rwkv_shift.astype(F32), state_gla_s.astype(F32))

    x = (x_prompt.reshape(mp, d), x_sample.reshape(ms, d))
    new_p = [[] for _ in range(6)]
    new_s = [[] for _ in range(6)]
    for l in range(depth):
        x = _ffn(x, *f1, l, tm_ffn, 512)
        proj = _inproj(x, mix_pre, w_in_p, l, tm_proj, P_PAD // 4)
        ys = []
        for (row0, B, T, nch, nb, st, sl, acc) in ((0, bp, tp, bp, 1, st_p, 0, new_p), (mp, bs, ts, 8, 4, st_s, l, new_s)):
            ym, C, n, mm = _mlstm(proj, row0, B, T, nb, st[0], st[1], st[2], sl, m_bias, m_norm, l)
            yr, Sr, sh = _rwkv(proj, row0, B, T, nch, st[3], sl, st[4][sl], rw, l)
            yg, Sg = _gla(proj, row0, B, T, nb, st[5], sl, g_aup, g_ab, g_norm, l)
            ys.append((ym, yr, yg))
            for j, a in enumerate((C, n, mm, Sr, sh, Sg)):
                acc[j].append(a)
        x = _outproj(x, ys[0], ys[1], w_out_b, mix_post, l, tm_out)
        x = _ffn(x, *f2, l, tm_ffn, 512, split_rows=mp if l == depth - 1 else None)

    dts = (state_mlstm_c.dtype, state_mlstm_n.dtype, state_mlstm_m.dtype,
           state_rwkv_s.dtype, state_rwkv_shift.dtype, state_gla_s.dtype)
    outs_p = [jnp.stack(a).astype(dt) for a, dt in zip(new_p, dts)]
    outs_s = [jnp.stack(a).astype(dt) for a, dt in zip(new_s, dts)]
    return (x[0].reshape(bp, tp, d), x[1].reshape(bs, ts, d), *outs_p, *outs_s)
```

```python
import functools
import math

import jax
import jax.numpy as jnp
from jax import lax
from jax.experimental import pallas as pl
from jax.experimental.pallas import tpu as pltpu

F32 = jnp.float32
BF16 = jnp.bfloat16

D_MODEL = 2048
D_FF = 5632
NORM_EPS = 1e-6
GN_EPS = 64e-5
GLA_TAU = 16.0
CHUNK = 64

H_M, DK_M, DV_M = 8, 64, 128
D_M = H_M * DV_M
H_R, N_R = 8, 64
D_R = H_R * N_R
W_LORA, A_LORA, G_LORA = 32, 32, 96
P_R = 3 * D_R + W_LORA + A_LORA + G_LORA
H_G, DK_G, DV_G = 4, 64, 128
D_G = H_G * DV_G
K_G = H_G * DK_G
ALPHA_LORA = 16
P_M = 2 * H_M * DK_M + 2 * D_M + 2 * H_M
P_G = 2 * K_G + 2 * D_G + ALPHA_LORA

C_QM, C_KM, C_VM, C_OM = 0, 512, 1024, 2048
C_RKV = 3072
C_GG, C_VG, C_QG, C_KG = 4608, 5120, 5632, 5888
C_LORA, LORA_W = 6144, 256
C_GATE, GATE_W = 6400, 128
C_XA, XA_W = 6528, 128
P_PAD = 6656

VMEM_LIMIT = 56 * 1024 * 1024
HI = lax.Precision.HIGHEST


def _dot(a, b):
    return jnp.dot(a.astype(BF16), b.astype(BF16), preferred_element_type=F32)


def _dot_nt(a, b):
    return lax.dot_general(a.astype(BF16), b.astype(BF16), (((1,), (1,)), ((), ())),
                           preferred_element_type=F32)


def _dot_tn(a, b):
    return lax.dot_general(a.astype(BF16), b.astype(BF16), (((0,), (0,)), ((), ())),
                           preferred_element_type=F32)


def _dot_hi(a, b):
    return jnp.dot(a, b, precision=HI, preferred_element_type=F32)


def _dot_nt_hi(a, b):
    return lax.dot_general(a, b, (((1,), (1,)), ((), ())), precision=HI,
                           preferred_element_type=F32)


def _rms(x, g):
    return x * lax.rsqrt(jnp.mean(x * x, axis=-1, keepdims=True) + NORM_EPS) * g


def _log_sigmoid(x):
    return jnp.minimum(x, 0.0) - jnp.log(1.0 + jnp.exp(-jnp.abs(x)))


def _softplus(x):
    return jnp.maximum(x, 0.0) + jnp.log(1.0 + jnp.exp(-jnp.abs(x)))


def _sigmoid(x):
    return 1.0 / (1.0 + jnp.exp(-x))


def _iota2(shape, dim):
    return lax.broadcasted_iota(jnp.int32, shape, dim)


def _tril_ones(n):
    return (_iota2((n, n), 1) <= _iota2((n, n), 0)).astype(F32)


def _ffn_body(nf, n_p, n_in, n_out, *refs):
    x_refs = refs[:n_in]
    pre_ref, post_ref, wg_ref, wu_ref, wd_ref = refs[n_in:n_in + 5]
    o_refs = refs[n_in + 5:n_in + 5 + n_out]
    h_ref = refs[-1]
    f = pl.program_id(1)

    def run(x_ref, o_ref):
        @pl.when(f == 0)
        def _():
            h_ref[...] = _rms(x_ref[...], pre_ref[...]).astype(BF16)
            o_ref[...] = jnp.zeros_like(o_ref)

        h = h_ref[...]
        g = jnp.dot(h, wg_ref[...], preferred_element_type=F32)
        u = jnp.dot(h, wu_ref[...], preferred_element_type=F32)
        a = (g * _sigmoid(g) * u).astype(BF16)
        o_ref[...] += jnp.dot(a, wd_ref[...], preferred_element_type=F32)

        @pl.when(f == nf - 1)
        def _():
            o_ref[...] = x_ref[...] + 0.5 * _rms(o_ref[...], post_ref[...])

    if n_in == 1 and n_out == 1:
        run(x_refs[0], o_refs[0])
    else:
        @pl.when(pl.program_id(0) < n_p)
        def _():
            run(x_refs[0], o_refs[0])

        @pl.when(pl.program_id(0) >= n_p)
        def _():
            run(x_refs[-1], o_refs[-1])


def _ffn(xs, pre_g, post_g, wg, wu, wd, layer, tm, tf, split_rows=None):
    xs = xs if isinstance(xs, tuple) else (xs,)
    d = xs[0].shape[1]
    m = sum(x.shape[0] for x in xs)
    n_p = (xs[0].shape[0] if len(xs) == 2 else (split_rows or 0)) // tm
    assert all(x.shape[0] % tm == 0 for x in xs) and (split_rows or 0) % tm == 0
    nf = D_FF // tf
    whole = pl.BlockSpec((tm, d), lambda i, f: (i, 0))
    first = pl.BlockSpec((tm, d), lambda i, f: (jnp.minimum(i, n_p - 1), 0))
    rest = pl.BlockSpec((tm, d), lambda i, f: (jnp.maximum(i - n_p, 0), 0))
    out_shape = jax.ShapeDtypeStruct((m, d), F32)
    out_specs = whole
    if split_rows:
        out_shape = [jax.ShapeDtypeStruct((split_rows, d), F32), jax.ShapeDtypeStruct((m - split_rows, d), F32)]
        out_specs = [first, rest]
    return pl.pallas_call(
        functools.partial(_ffn_body, nf, n_p, len(xs), 2 if split_rows else 1),
        grid=(m // tm, nf),
        in_specs=([first, rest] if len(xs) == 2 else [whole]) + [
            pl.BlockSpec((None, 1, d), lambda i, f: (layer, 0, 0)),
            pl.BlockSpec((None, 1, d), lambda i, f: (layer, 0, 0)),
            pl.BlockSpec((None, d, tf), lambda i, f: (layer, 0, f)),
            pl.BlockSpec((None, d, tf), lambda i, f: (layer, 0, f)),
            pl.BlockSpec((None, tf, d), lambda i, f: (layer, f, 0)),
        ],
        out_specs=out_specs,
        out_shape=out_shape,
        scratch_shapes=[pltpu.VMEM((tm, d), BF16)],
        compiler_params=pltpu.CompilerParams(
            dimension_semantics=("arbitrary", "arbitrary"), vmem_limit_bytes=VMEM_LIMIT),
        name="ffn",
    )(*xs, pre_g, post_g, wg, wu, wd)


def _inproj_body(x_ref, g_ref, w_ref, o_ref, h_ref):
    @pl.when(pl.program_id(1) == 0)
    def _():
        h_ref[...] = _rms(x_ref[...], g_ref[...]).astype(BF16)

    o_ref[...] = jnp.dot(h_ref[...], w_ref[...], preferred_element_type=F32)


def _inproj(x, pre_g, w_in, layer, tm, tn):
    m, d = x.shape
    return pl.pallas_call(
        _inproj_body,
        grid=(m // tm, P_PAD // tn),
        in_specs=[
            pl.BlockSpec((tm, d), lambda i, j: (i, 0)),
            pl.BlockSpec((None, 1, d), lambda i, j: (layer, 0, 0)),
            pl.BlockSpec((None, d, tn), lambda i, j: (layer, 0, j)),
        ],
        out_specs=pl.BlockSpec((tm, tn), lambda i, j: (i, j)),
        out_shape=jax.ShapeDtypeStruct((m, P_PAD), F32),
        scratch_shapes=[pltpu.VMEM((tm, d), BF16)],
        compiler_params=pltpu.CompilerParams(
            dimension_semantics=("arbitrary", "arbitrary"), vmem_limit_bytes=VMEM_LIMIT),
        name="inproj",
    )(x, pre_g, w_in)


def _outproj_body(n_p, x_ref, ymp_ref, yrp_ref, ygp_ref, yms_ref, yrs_ref, ygs_ref, w_ref, g_ref, o_ref):
    def run(ym_ref, yr_ref, yg_ref):
        y = jnp.dot(ym_ref[...].astype(BF16), w_ref[0:D_M, :], preferred_element_type=F32)
        y += jnp.dot(yr_ref[...].astype(BF16), w_ref[D_M:D_M + D_R, :], preferred_element_type=F32)
        y += jnp.dot(yg_ref[...].astype(BF16), w_ref[D_M + D_R:, :], preferred_element_type=F32)
        o_ref[...] = x_ref[...] + _rms(y, g_ref[...])

    @pl.when(pl.program_id(0) < n_p)
    def _():
        run(ymp_ref, yrp_ref, ygp_ref)

    @pl.when(pl.program_id(0) >= n_p)
    def _():
        run(yms_ref, yrs_ref, ygs_ref)


def _outproj(x, y_p, y_s, w_out, post_g, layer, tm):
    m, d = x.shape
    n_p = y_p[0].shape[0] // tm
    assert y_p[0].shape[0] % tm == 0 and y_s[0].shape[0] % tm == 0
    pspec = lambda w: pl.BlockSpec((tm, w), lambda i: (jnp.minimum(i, n_p - 1), 0))
    sspec = lambda w: pl.BlockSpec((tm, w), lambda i: (jnp.maximum(i - n_p, 0), 0))
    return pl.pallas_call(
        functools.partial(_outproj_body, n_p),
        grid=(m // tm,),
        in_specs=[
            pl.BlockSpec((tm, d), lambda i: (i, 0)),
            pspec(D_M), pspec(D_R), pspec(D_G), sspec(D_M), sspec(D_R), sspec(D_G),
            pl.BlockSpec((None, d, d), lambda i: (layer, 0, 0)),
            pl.BlockSpec((None, 1, d), lambda i: (layer, 0, 0)),
        ],
        out_specs=pl.BlockSpec((tm, d), lambda i: (i, 0)),
        out_shape=jax.ShapeDtypeStruct((m, d), F32),
        compiler_params=pltpu.CompilerParams(
            dimension_semantics=("arbitrary",), vmem_limit_bytes=VMEM_LIMIT),
        name="outproj",
    )(x, *y_p, *y_s, w_out, post_g)


def _mlstm_body(L, nc, nb, q_ref, k_ref, v_ref, o_ref, gt_ref, c0_ref, n0_ref, m0_ref, bias_ref, ng_ref,
                y_ref, c_out, n_out, m_out, c_s, n_s, m_s):
    c = pl.program_id(1)

    @pl.when(c == 0)
    def _():
        c_s[...] = c0_ref[...]
        n_s[...] = n0_ref[...]
        m_s[...] = m0_ref[:, 0, :]

    eye = (_iota2((GATE_W, GATE_W), 0) == _iota2((GATE_W, GATE_W), 1)).astype(F32)
    causal = _iota2((L, L), 1) <= _iota2((L, L), 0)
    tril = _tril_ones(L)
    seqs = range(nb)
    rows = [slice(i * L, (i + 1) * L) for i in seqs]

    gates, gates_t = [], []
    for i in seqs:
        z = gt_ref[rows[i], :] + bias_ref[...]
        lane = _iota2(z.shape, 1)
        is_f = (lane >= H_M) & (lane < 2 * H_M)
        lf = jnp.where(is_f, _log_sigmoid(z), 0.0)
        gates.append(jnp.where(lane < H_M, z, _dot_hi(tril, lf)))
        gates_t.append(_dot_nt_hi(eye, gates[i]))

    units = [(i, h) for i in seqs for h in range(H_M)]
    c_old = [c_s[i, h] for i, h in units]
    n_old = [n_s[i] for i in seqs]
    m_old = [m_s[i:i + 1, :] for i in seqs]
    q = [q_ref[rows[i], h * DK_M:(h + 1) * DK_M] for i, h in units]
    k = [k_ref[rows[i], h * DK_M:(h + 1) * DK_M] * (DK_M ** -0.5) for i, h in units]
    v = [v_ref[rows[i], h * DV_M:(h + 1) * DV_M] for i, h in units]
    qk = [_dot_nt(q[u], k[u]) for u in range(len(units))]
    qc = [_dot(q[u], c_old[u]) for u in range(len(units))]

    dexp, w_inter, m_t, kw, sc = [], [], [], [], []
    m_new_all = [jnp.zeros_like(m_old[i]) for i in seqs]
    for u, (i, h) in enumerate(units):
        li_c = gates[i][:, h:h + 1]
        b_c = gates[i][:, H_M + h:H_M + h + 1]
        li_r = gates_t[i][h:h + 1, :]
        b_r = gates_t[i][H_M + h:H_M + h + 1, :]
        m_prev = m_old[i][:, h:h + 1]
        dmat = jnp.where(causal, b_c - b_r + li_r, -jnp.inf)
        inter = b_c + m_prev
        m_t.append(jnp.maximum(inter, jnp.max(dmat, axis=-1, keepdims=True)))
        dexp.append(jnp.exp(dmat - m_t[u]))
        w_inter.append(jnp.exp(inter - m_t[u]))
        g_end = b_c[L - 1:L, :]
        dec = g_end - b_c + li_c
        m_new = jnp.maximum(g_end + m_prev, jnp.max(dec, axis=0, keepdims=True))
        kw.append(k[u] * jnp.exp(dec - m_new))
        sc.append(jnp.exp(g_end + m_prev - m_new))
        m_new_all[i] = jnp.where(_iota2(m_old[i].shape, 1) == h, m_new, m_new_all[i])

    us = range(len(units))
    kv = [_dot_tn(kw[u], v[u]) for u in us]
    w_intra = [dexp[u] * qk[u] for u in us]
    ones_v = jnp.ones((L, DV_M), F32)
    wv = [_dot(w_intra[u], jnp.concatenate([v[u], ones_v], axis=1)) for u in us]
    n_rows = [n_old[i][h:h + 1, :] for i, h in units]
    qn = [_dot_nt(q[u], jnp.broadcast_to(n_rows[u], (DV_M, DK_M))) for u in us]
    hh = []
    for u in us:
        num = w_inter[u] * qc[u] + wv[u][:, :DV_M]
        den = w_inter[u] * qn[u] + wv[u][:, DV_M:]
        hh.append(num / jnp.maximum(jnp.abs(den), jnp.exp(-m_t[u])))
    ones_m = jnp.ones((DV_M, DV_M), F32)
    ms = [_dot(hh[u] * hh[u], ones_m) * (1.0 / DV_M) for u in us]

    c_new, y_new = [], []
    n_new = [jnp.zeros_like(n_old[i]) for i in seqs]
    for u, (i, h) in enumerate(units):
        c_new.append(sc[u] * c_old[u] + kv[u])
        n_new[i] = jnp.where(_iota2(n_old[i].shape, 0) == h,
                             sc[u] * n_rows[u] + jnp.sum(kw[u], axis=0, keepdims=True), n_new[i])
        sl = slice(h * DV_M, (h + 1) * DV_M)
        y_new.append(hh[u] * lax.rsqrt(ms[u] + NORM_EPS) * ng_ref[:, sl] * _sigmoid(o_ref[rows[i], sl]))

    for u, (i, h) in enumerate(units):
        c_s[i, h] = c_new[u]
        y_ref[rows[i], h * DV_M:(h + 1) * DV_M] = y_new[u]
    for i in seqs:
        n_s[i] = n_new[i]
        m_s[i:i + 1, :] = m_new_all[i]

    @pl.when(c == nc - 1)
    def _():
        c_out[...] = c_s[...]
        n_out[...] = n_s[...]
        m_out[:, 0, :] = m_s[...]


def _mlstm(proj, row0, B, T, nb, c_all, n_all, m_all, sl, bias, norm_g, layer):
    L = math.gcd(T, CHUNK)
    nc = T // L
    R = nb * L
    assert (nb == 1 or nc == 1) and B % nb == 0 and row0 % R == 0
    rb = row0 // R
    row = lambda w, j: pl.BlockSpec((R, w), lambda g, c: (rb + g * nc + c, j))
    y, c_new, n_new, m_new = pl.pallas_call(
        functools.partial(_mlstm_body, L, nc, nb),
        grid=(B // nb, nc),
        in_specs=[
            row(512, C_QM // 512), row(512, C_KM // 512), row(D_M, C_VM // D_M), row(D_M, C_OM // D_M),
            row(GATE_W, C_GATE // GATE_W),
            pl.BlockSpec((None, nb, H_M, DK_M, DV_M), lambda g, c: (sl, g, 0, 0, 0)),
            pl.BlockSpec((None, nb, H_M, DK_M), lambda g, c: (sl, g, 0, 0)),
            pl.BlockSpec((None, nb, 1, GATE_W), lambda g, c: (sl, g, 0, 0)),
            pl.BlockSpec((None, 1, GATE_W), lambda g, c: (layer, 0, 0)),
            pl.BlockSpec((None, 1, D_M), lambda g, c: (layer, 0, 0)),
        ],
        out_specs=[
            pl.BlockSpec((R, D_M), lambda g, c: (g * nc + c, 0)),
            pl.BlockSpec((nb, H_M, DK_M, DV_M), lambda g, c: (g, 0, 0, 0)),
            pl.BlockSpec((nb, H_M, DK_M), lambda g, c: (g, 0, 0)),
            pl.BlockSpec((nb, 1, GATE_W), lambda g, c: (g, 0, 0)),
        ],
        out_shape=[
            jax.ShapeDtypeStruct((B * T, D_M), F32),
            jax.ShapeDtypeStruct((B, H_M, DK_M, DV_M), F32),
            jax.ShapeDtypeStruct((B, H_M, DK_M), F32),
            jax.ShapeDtypeStruct((B, 1, GATE_W), F32),
        ],
        scratch_shapes=[pltpu.VMEM((nb, H_M, DK_M, DV_M), F32), pltpu.VMEM((nb, H_M, DK_M), F32),
                        pltpu.VMEM((nb, GATE_W), F32)],
        compiler_params=pltpu.CompilerParams(
            dimension_semantics=("arbitrary", "arbitrary"), vmem_limit_bytes=VMEM_LIMIT),
        name="mlstm",
    )(proj, proj, proj, proj, proj, c_all, n_all, m_all, bias, norm_g)
    return y, c_new, n_new, m_new[:, 0, :H_M]


def _seg_sum(x, e_ref, passes):
    xh = x.astype(BF16)
    parts = [xh]
    if passes == 2:
        parts.append((x - xh.astype(F32)).astype(BF16))
    e = e_ref[...]
    halves = []
    for j in range(x.shape[1] // 256):
        sl = slice(j * 256, (j + 1) * 256)
        halves.append(sum(jnp.dot(p[:, sl], e, preferred_element_type=F32) for p in parts))
    return jnp.concatenate(halves, axis=1)


def _rwkv_body(L, nc, nch, n_src, *refs):
    rkv_refs, lo_refs = refs[:n_src], refs[n_src:2 * n_src]
    (s0_ref, sh_rkv_ref, sh_lo_ref, mu_rkv_ref, mu_lo_ref, w0_ref, a0_ref, wup_ref, aup_ref, gup_ref,
     kk_ref, ka_ref, rk_ref, lnw_ref, lnb_ref, e_ref, diag_ref,
     y_ref, s_out, last_rkv, last_lo,
     s_s, car_rkv, car_lo, nkk_s, w_s, b_s, k_s, v_s, r_s, yy_s) = refs[2 * n_src:]
    c = pl.program_id(1)

    @pl.when(c == 0)
    def _():
        for j in range(nch):
            for h in range(H_R):
                s_s[j, :, h * N_R:(h + 1) * N_R] = s0_ref[j, h]
        car_rkv[...] = sh_rkv_ref[...]
        car_lo[...] = sh_lo_ref[...]

    def slab(src):
        return src[0][...] if n_src == 1 else jnp.concatenate([r[...] for r in src], axis=0)

    def shift(x, car_ref, mu):
        rolled = pltpu.roll(x, shift=1, axis=0)
        car = jnp.concatenate(
            [jnp.broadcast_to(car_ref[j:j + 1, :], (L, x.shape[1])) for j in range(nch)], axis=0)
        prev = jnp.where(_iota2(x.shape, 0) % L == 0, car, rolled)
        return x + (prev - x) * mu

    rkv = slab(rkv_refs)
    lo = slab(lo_refs)
    xs = shift(rkv, car_rkv, mu_rkv_ref[...])
    xl = shift(lo, car_lo, mu_lo_ref[...])
    for j in range(nch):
        car_rkv[j:j + 1, :] = rkv[(j + 1) * L - 1:(j + 1) * L, :]
        car_lo[j:j + 1, :] = lo[(j + 1) * L - 1:(j + 1) * L, :]

    r = xs[:, 0:D_R]
    k = xs[:, D_R:2 * D_R]
    v = xs[:, 2 * D_R:3 * D_R]
    w = -_softplus(-(w0_ref[...] + _dot(jnp.tanh(xl), wup_ref[...]))) - 0.5
    decay = jnp.exp(-jnp.exp(w))
    a = _sigmoid(a0_ref[...] + _dot(xl, aup_ref[...]))
    g = _dot(_sigmoid(xl), gup_ref[...])
    kk = k * kk_ref[...]
    kk = kk / jnp.maximum(jnp.sqrt(_seg_sum(kk * kk, e_ref, 2)), 1e-12)
    k = k * (1.0 + (a - 1.0) * ka_ref[...])

    nkk_s[...] = -kk
    w_s[...] = decay
    b_s[...] = kk * a
    k_s[...] = k
    v_s[...] = v
    r_s[...] = r

    diag = diag_ref[...]
    diag_b = diag.astype(BF16)
    e = e_ref[...]

    def head_sums(lhs):
        return jnp.concatenate(
            [jnp.dot(lhs[:, i * 256:(i + 1) * 256], e, preferred_element_type=F32)
             for i in range(D_R // 256)], axis=1)

    def put_y(j, t, yexp):
        yy_s[pl.ds(8 + j * L + t, 1), :] = jnp.sum(yexp * diag, axis=0, keepdims=True)

    def step(t, carry):
        tp = jnp.maximum(t - 1, 0)
        parts = []
        for j in range(nch):
            rowb = lambda ref, tt: ref[pl.ds(j * L + tt, 1), :].astype(BF16)
            sb = s_s[j].astype(BF16)
            parts += [sb * rowb(nkk_s, t), diag_b * rowb(v_s, t), sb * rowb(r_s, tp)]
        res = head_sums(jnp.concatenate(parts, axis=0))
        for j in range(nch):
            row = lambda ref: ref[pl.ds(j * L + t, 1), :]
            sa, vexp, yexp = (res[(3 * j + i) * N_R:(3 * j + i + 1) * N_R] for i in range(3))
            s_s[j] = s_s[j] * row(w_s) + sa * row(b_s) + vexp * row(k_s)
            put_y(j, t - 1, yexp)
        return carry

    lax.fori_loop(0, L, step, 0, unroll=8)
    last = jnp.concatenate(
        [s_s[j].astype(BF16) * r_s[(j + 1) * L - 1:(j + 1) * L, :].astype(BF16) for j in range(nch)], axis=0)
    res = head_sums(last)
    for j in range(nch):
        put_y(j, L - 1, res[j * N_R:(j + 1) * N_R])

    y = yy_s[8:, :]
    mean = _seg_sum(y, e_ref, 2) * (1.0 / N_R)
    d = y - mean
    var = _seg_sum(d * d, e_ref, 2) * (1.0 / N_R)
    y = d * lax.rsqrt(var + GN_EPS) * lnw_ref[...] + lnb_ref[...]
    y = (y + _seg_sum(r * k * rk_ref[...], e_ref, 2) * v) * g
    for j in range(nch):
        y_ref[j] = y[j * L:(j + 1) * L, :]

    @pl.when(c == nc - 1)
    def _():
        for j in range(nch):
            for h in range(H_R):
                s_out[j, h] = s_s[j, :, h * N_R:(h + 1) * N_R]
            last_rkv[j:j + 1, :] = rkv[(j + 1) * L - 1:(j + 1) * L, :]
            last_lo[j:j + 1, :] = lo[(j + 1) * L - 1:(j + 1) * L, :]


def _rwkv(proj, row0, B, T, nch, s_all, sl, shift0, p, layer):
    L = math.gcd(T, CHUNK)
    nc = T // L
    assert B % nch == 0
    sh_r, sh_xw, sh_k, sh_v, sh_xa, sh_xg = jnp.split(
        shift0, [D_R, D_R + W_LORA, 2 * D_R + W_LORA, 3 * D_R + W_LORA, 3 * D_R + W_LORA + A_LORA], axis=-1)
    sh_rkv = jnp.concatenate([sh_r, sh_k, sh_v], axis=-1)
    sh_lo = jnp.concatenate(
        [sh_xw, sh_xa, sh_xg, jnp.zeros((B, LORA_W - W_LORA - A_LORA - G_LORA), F32)], axis=-1)
    lane = jnp.arange(256)
    e256 = (lane[:, None] // N_R == lane[None, :] // N_R).astype(BF16)
    diag = (jnp.arange(N_R)[:, None] == (jnp.arange(D_R)[None, :] % N_R)).astype(F32)
    per_layer = lambda a: pl.BlockSpec((None,) + a.shape[1:], lambda g, c: (layer,) + (0,) * (a.ndim - 1))
    const = lambda a: pl.BlockSpec(a.shape, lambda g, c: (0,) * a.ndim)
    params = [p["mu_rkv"], p["mu_lo"], p["w0"], p["a0"], p["w_up"], p["a_up"], p["g_up"],
              p["k_k"], p["k_a"], p["r_k"], p["ln_w"], p["ln_b"]]
    if nc == 1:
        rows = nch * L
        assert row0 % rows == 0
        n_src = 1
        src = lambda w, col: [pl.BlockSpec((rows, w), lambda g, c: (row0 // rows + g, col // w))]
    else:
        n_src = nch
        src = lambda w, col: [
            pl.BlockSpec((L, w), lambda g, c, j=j: (row0 // L + (g * nch + j) * nc + c, col // w))
            for j in range(nch)]
    y, s_new, last_rkv, last_lo = pl.pallas_call(
        functools.partial(_rwkv_body, L, nc, nch, n_src),
        grid=(B // nch, nc),
        in_specs=src(3 * D_R, C_RKV) + src(LORA_W, C_LORA) + [
            pl.BlockSpec((None, nch, H_R, N_R, N_R), lambda g, c: (sl, g, 0, 0, 0)),
            pl.BlockSpec((nch, 3 * D_R), lambda g, c: (g, 0)),
            pl.BlockSpec((nch, LORA_W), lambda g, c: (g, 0)),
        ] + [per_layer(a) for a in params] + [const(e256), const(diag)],
        out_specs=[
            pl.BlockSpec((nch, L, D_R), lambda g, c: (g, c, 0)),
            pl.BlockSpec((nch, H_R, N_R, N_R), lambda g, c: (g, 0, 0, 0)),
            pl.BlockSpec((nch, 3 * D_R), lambda g, c: (g, 0)),
            pl.BlockSpec((nch, LORA_W), lambda g, c: (g, 0)),
        ],
        out_shape=[
            jax.ShapeDtypeStruct((B, T, D_R), F32),
            jax.ShapeDtypeStruct((B, H_R, N_R, N_R), F32),
            jax.ShapeDtypeStruct((B, 3 * D_R), F32),
            jax.ShapeDtypeStruct((B, LORA_W), F32),
        ],
        scratch_shapes=[pltpu.VMEM((nch, N_R, D_R), F32), pltpu.VMEM((nch, 3 * D_R), F32),
                        pltpu.VMEM((nch, LORA_W), F32)]
        + [pltpu.VMEM((nch * L, D_R), F32)] * 6 + [pltpu.VMEM((nch * L + 8, D_R), F32)],
        compiler_params=pltpu.CompilerParams(
            dimension_semantics=("arbitrary", "arbitrary"), vmem_limit_bytes=VMEM_LIMIT),
        name="rwkv",
    )(*([proj] * (2 * n_src)), s_all, sh_rkv, sh_lo, *params, e256, diag)
    shift_new = jnp.concatenate(
        [last_rkv[:, 0:D_R], last_lo[:, 0:W_LORA], last_rkv[:, D_R:3 * D_R],
         last_lo[:, W_LORA:W_LORA + A_LORA + G_LORA]], axis=-1)
    return y.reshape(B * T, D_R), s_new, shift_new


def _gla_body(L, nc, nb, q_ref, k_ref, v_ref, gg_ref, xa_ref, s0_ref, aup_ref, ab_ref, ng_ref,
              y_ref, s_out, s_s, bc_s):
    c = pl.program_id(1)

    @pl.when(c == 0)
    def _():
        s_s[...] = s0_ref[...]

    la = _log_sigmoid(_dot(xa_ref[...], aup_ref[...]) + ab_ref[...]) * (1.0 / GLA_TAU)
    tril = _tril_ones(L)
    for i in range(nb):
        bc_s[i * L:(i + 1) * L, :] = _dot_hi(tril, la[i * L:(i + 1) * L, :])
    ls = min(L, 16)
    eye = (_iota2((DK_G, DK_G), 0) == _iota2((DK_G, DK_G), 1)).astype(F32)

    units = [(i, h) for i in range(nb) for h in range(H_G)]
    us = range(len(units))
    rows = [slice(i * L, (i + 1) * L) for i, _ in units]
    ksl = [slice(h * DK_G, (h + 1) * DK_G) for _, h in units]
    vsl = [slice(h * DV_G, (h + 1) * DV_G) for _, h in units]
    s_old = [s_s[i, h] for i, h in units]
    q = [q_ref[rows[u], ksl[u]] * (DK_G ** -0.5) for u in us]
    k = [k_ref[rows[u], ksl[u]] for u in us]
    bc = [bc_s[rows[u], ksl[u]] for u in us]
    b_end = [bc_s[units[u][0] * L + L - 1:units[u][0] * L + L, ksl[u]] for u in us]
    inter = [_dot(q[u] * jnp.exp(bc[u]), s_old[u]) for u in us]
    kv = [_dot_tn(k[u] * jnp.exp(b_end[u] - bc[u]), v_ref[rows[u], vsl[u]]) for u in us]
    sdec = [_dot_hi(eye * jnp.exp(b_end[u]), s_old[u]) for u in us]
    nsub = L // ls
    att = [[None] * nsub for _ in us]
    for j in range(nsub):
        lo, hi = j * ls, (j + 1) * ls
        for u in us:
            r0 = units[u][0] * L
            rho = bc_s[r0 + lo - 1:r0 + lo, ksl[u]] if j > 0 else jnp.zeros((1, DK_G), F32)
            qi = q[u][lo:hi] * jnp.exp(bc[u][lo:hi] - rho)
            ki = k[u][0:hi] * jnp.exp(rho - bc[u][0:hi])
            a = _dot_nt(qi, ki)
            att[u][j] = jnp.where(_iota2(a.shape, 1) <= _iota2(a.shape, 0) + lo, a, 0.0)
    intra = [[_dot(att[u][j], v_ref[units[u][0] * L:units[u][0] * L + (j + 1) * ls, vsl[u]])
              for j in range(nsub)] for u in us]
    s_new, y_new = [], []
    for u in us:
        o = inter[u] + jnp.concatenate(intra[u], axis=0)
        s_new.append(sdec[u] + kv[u])
        gg = gg_ref[rows[u], vsl[u]]
        y_new.append(_rms(o, ng_ref[:, vsl[u]]) * (gg * _sigmoid(gg)))

    for u, (i, h) in enumerate(units):
        s_s[i, h] = s_new[u]
        y_ref[rows[u], vsl[u]] = y_new[u]

    @pl.when(c == nc - 1)
    def _():
        s_out[...] = s_s[...]


def _gla(proj, row0, B, T, nb, s_all, sl, alpha_up, alpha_b, norm_g, layer):
    L = math.gcd(T, CHUNK)
    nc = T // L
    R = nb * L
    assert (nb == 1 or nc == 1) and B % nb == 0 and row0 % R == 0
    rb = row0 // R
    row = lambda w, j: pl.BlockSpec((R, w), lambda g, c: (rb + g * nc + c, j))
    y, s_new = pl.pallas_call(
        functools.partial(_gla_body, L, nc, nb),
        grid=(B // nb, nc),
        in_specs=[
            row(K_G, C_QG // K_G), row(K_G, C_KG // K_G), row(D_G, C_VG // D_G), row(D_G, C_GG // D_G),
            row(XA_W, C_XA // XA_W),
            pl.BlockSpec((None, nb, H_G, DK_G, DV_G), lambda g, c: (sl, g, 0, 0, 0)),
            pl.BlockSpec((None, XA_W, K_G), lambda g, c: (layer, 0, 0)),
            pl.BlockSpec((None, 1, K_G), lambda g, c: (layer, 0, 0)),
            pl.BlockSpec((None, 1, D_G), lambda g, c: (layer, 0, 0)),
        ],
        out_specs=[
            pl.BlockSpec((R, D_G), lambda g, c: (g * nc + c, 0)),
            pl.BlockSpec((nb, H_G, DK_G, DV_G), lambda g, c: (g, 0, 0, 0)),
        ],
        out_shape=[
            jax.ShapeDtypeStruct((B * T, D_G), F32),
            jax.ShapeDtypeStruct((B, H_G, DK_G, DV_G), F32),
        ],
        scratch_shapes=[pltpu.VMEM((nb, H_G, DK_G, DV_G), F32), pltpu.VMEM((R, K_G), F32)],
        compiler_params=pltpu.CompilerParams(
            dimension_semantics=("arbitrary", "arbitrary"), vmem_limit_bytes=VMEM_LIMIT),
        name="gla",
    )(proj, proj, proj, proj, proj, s_all, alpha_up, alpha_b, norm_g)
    return y, s_new


def _permute_cols(w_in):
    pm, pr, pg = jnp.split(w_in, [P_M, P_M + P_R], axis=-1)
    qm, km, vm, om, gates = jnp.split(pm, [512, 1024, 2048, 3072], axis=-1)
    r, xw, k, v, xa, xg = jnp.split(
        pr, [D_R, D_R + W_LORA, 2 * D_R + W_LORA, 3 * D_R + W_LORA, 3 * D_R + W_LORA + A_LORA], axis=-1)
    qg, kg, vg, xag, gg = jnp.split(pg, [K_G, 2 * K_G, 2 * K_G + D_G, 2 * K_G + D_G + ALPHA_LORA], axis=-1)
    zeros = lambda n: jnp.zeros(w_in.shape[:-1] + (n,), w_in.dtype)
    cols = [qm, km, vm, om, r, k, v, gg, vg, qg, kg,
            xw, xa, xg, zeros(LORA_W - W_LORA - A_LORA - G_LORA),
            gates, zeros(GATE_W - 2 * H_M), xag, zeros(XA_W - ALPHA_LORA)]
    return jnp.concatenate(cols, axis=-1)


def _rows_at(w, start, total):
    n = w.shape[1]
    return jnp.pad(w, ((0, 0), (start, total - start - n), (0, 0))).astype(BF16)


def _prep_rwkv(mu, w0, w_up, a0, a_up, g_up, k_k, k_a, r_k, ln_w, ln_b):
    depth = mu.shape[0]
    mu_r, mu_xw, mu_k, mu_v, mu_xa, mu_xg = jnp.split(
        mu, [D_R, D_R + W_LORA, 2 * D_R + W_LORA, 3 * D_R + W_LORA, 3 * D_R + W_LORA + A_LORA], axis=-1)
    row = lambda a: a.reshape(depth, 1, -1).astype(F32)
    return dict(
        mu_rkv=row(jnp.concatenate([mu_r, mu_k, mu_v], axis=-1)),
        mu_lo=row(jnp.concatenate(
            [mu_xw, mu_xa, mu_xg, jnp.zeros((depth, LORA_W - W_LORA - A_LORA - G_LORA), mu.dtype)], axis=-1)),
        w0=row(w0), a0=row(a0),
        w_up=_rows_at(w_up, 0, LORA_W),
        a_up=_rows_at(a_up, W_LORA, LORA_W),
        g_up=_rows_at(g_up, W_LORA + A_LORA, LORA_W),
        k_k=row(k_k), k_a=row(k_a), r_k=row(r_k), ln_w=row(ln_w), ln_b=row(ln_b),
    )


def _pad_m(m):
    return jnp.pad(m.astype(F32), ((0, 0), (0, 0), (0, GATE_W - H_M)))[:, :, None, :]


def kernel(x_prompt, x_sample, state_mlstm_c, state_mlstm_n, state_mlstm_m, state_rwkv_s, state_rwkv_shift, state_gla_s, ffn1_pre_g, ffn1_post_g, ffn1_w_gate, ffn1_w_up, ffn1_w_down, mix_pre_g, mix_post_g, w_in, w_out, mlstm_b_i, mlstm_b_f, mlstm_norm_g, rwkv_mu, rwkv_w0, rwkv_w_up, rwkv_a0, rwkv_a_up, rwkv_g_up, rwkv_k_k, rwkv_k_a, rwkv_r_k, rwkv_ln_w, rwkv_ln_b, gla_alpha_up, gla_alpha_b, gla_norm_g, ffn2_pre_g, ffn2_post_g, ffn2_w_gate, ffn2_w_up, ffn2_w_down):
    depth = w_in.shape[0]
    bp, tp, d = x_prompt.shape
    bs, ts, _ = x_sample.shape
    mp, ms = bp * tp, bs * ts
    m = mp + ms
    tm_ffn, tm_proj, tm_out = 512, 1024, 512
    assert m % tm_ffn == 0 and m % tm_proj == 0 and mp % tm_out == 0 and ms % tm_out == 0

    row = lambda a: a.reshape(depth, 1, -1).astype(F32)
    bf = lambda a: a.astype(BF16)
    f1 = (row(ffn1_pre_g), row(ffn1_post_g), bf(ffn1_w_gate), bf(ffn1_w_up), bf(ffn1_w_down))
    f2 = (row(ffn2_pre_g), row(ffn2_post_g), bf(ffn2_w_gate), bf(ffn2_w_up), bf(ffn2_w_down))
    w_in_p = _permute_cols(w_in).astype(BF16)
    w_out_b = bf(w_out)
    mix_pre, mix_post = row(mix_pre_g), row(mix_post_g)
    m_bias = row(jnp.concatenate(
        [mlstm_b_i, mlstm_b_f, jnp.zeros((depth, GATE_W - 2 * H_M), mlstm_b_i.dtype)], axis=-1))
    m_norm = row(mlstm_norm_g)
    rw = _prep_rwkv(rwkv_mu, rwkv_w0, rwkv_w_up, rwkv_a0, rwkv_a_up, rwkv_g_up, rwkv_k_k, rwkv_k_a,
                    rwkv_r_k, rwkv_ln_w, rwkv_ln_b)
    g_aup = _rows_at(gla_alpha_up, 0, XA_W)
    g_ab, g_norm = row(gla_alpha_b), row(gla_norm_g)

    zp = lambda *s: jnp.zeros((1, bp) + s, F32)
    st_p = (zp(H_M, DK_M, DV_M), zp(H_M, DK_M), _pad_m(zp(H_M)), zp(H_R, N_R, N_R), zp(P_R), zp(H_G, DK_G, DV_G))
    st_s = (state_mlstm_c.astype(F32), state_mlstm_n.astype(F32), _pad_m(state_mlstm_m),
            state_rwkv_s.astype(F32), state_rwkv_shift.astype(F32), state_gla_s.astype(F32))

    x = (x_prompt.reshape(mp, d), x_sample.reshape(ms, d))
    new_p = [[] for _ in range(6)]
    new_s = [[] for _ in range(6)]
    for l in range(depth):
        x = _ffn(x, *f1, l, tm_ffn, 512)
        proj = _inproj(x, mix_pre, w_in_p, l, tm_proj, P_PAD // 4)
        ys = []
        for (row0, B, T, nch, nb, st, sl, acc) in ((0, bp, tp, bp, 1, st_p, 0, new_p), (mp, bs, ts, 8, 4, st_s, l, new_s)):
            ym, C, n, mm = _mlstm(proj, row0, B, T, nb, st[0], st[1], st[2], sl, m_bias, m_norm, l)
            yr, Sr, sh = _rwkv(proj, row0, B, T, nch, st[3], sl, st[4][sl], rw, l)
            yg, Sg = _gla(proj, row0, B, T, nb, st[5], sl, g_aup, g_ab, g_norm, l)
            ys.append((ym, yr, yg))
            for j, a in enumerate((C, n, mm, Sr, sh, Sg)):
                acc[j].append(a)
        x = _outproj(x, ys[0], ys[1], w_out_b, mix_post, l, tm_out)
        x = _ffn(x, *f2, l, tm_ffn, 512, split_rows=mp if l == depth - 1 else None)

    dts = (state_mlstm_c.dtype, state_mlstm_n.dtype, state_mlstm_m.dtype,
           state_rwkv_s.dtype, state_rwkv_shift.dtype, state_gla_s.dtype)
    outs_p = [jnp.stack(a).astype(dt) for a, dt in zip(new_p, dts)]
    outs_s = [jnp.stack(a).astype(dt) for a, dt in zip(new_s, dts)]
    return (x[0].reshape(bp, tp, d), x[1].reshape(bs, ts, d), *outs_p, *outs_s)
```
